```python
import math
import jax, jax.numpy as jnp
from jax import lax
import numpy as np

D_MODEL = 1024
BATCH = 32
SEQ = 2048
DEPTH = 2

CHUNK = 64
Q_BLOCK = 128
HEAD_DIM = 64
N_HEADS_TOTAL = D_MODEL // HEAD_DIM
N_HEADS_B = N_HEADS_TOTAL // 4
N_HEADS_A = (N_HEADS_TOTAL - N_HEADS_B) // 2
N_HEADS_C = N_HEADS_TOTAL - N_HEADS_A - N_HEADS_B
DIFF_QK_DIM = HEAD_DIM // 2
DIFF_V_DIM = HEAD_DIM
WIDTH_A = N_HEADS_A * HEAD_DIM
WIDTH_B = N_HEADS_B * DIFF_V_DIM
WIDTH_C = N_HEADS_C * HEAD_DIM
B_QK_WIDTH = N_HEADS_B * 2 * DIFF_QK_DIM
D_MIX = WIDTH_A + WIDTH_B + WIDTH_C
IN_COLS = 3 * WIDTH_A + 2 * B_QK_WIDTH + WIDTH_B + 3 * WIDTH_C
LEFT_CHUNKS = 8
BAND = (LEFT_CHUNKS + 1) * CHUNK
REL_CLIP = 128
D_FF = 2816
NORM_EPS = 1e-6
NEG_INF = -1e30
FFN_RESIDUAL_WEIGHT = 0.5

kernel_name = "hybrid_chunk_causal_hymba_encoder"


def rmsnorm(x, g):
    xf = x.astype(jnp.float32)
    y = xf * lax.rsqrt(jnp.mean(xf * xf, axis=-1, keepdims=True) + NORM_EPS)
    return (y * g.astype(jnp.float32)).astype(x.dtype)


def swiglu(x, w_gate, w_up, w_down):
    return (jax.nn.silu(x @ w_gate) * (x @ w_up)) @ w_down


def chunk_band_attention(q, k, v, rel_bias):
    b, s, h, d = q.shape
    nc = s // CHUNK
    qc = q.reshape(b, nc, CHUNK, h, d)
    pad = ((0, 0), (LEFT_CHUNKS, 0), (0, 0), (0, 0), (0, 0))
    kp = jnp.pad(k.reshape(b, nc, CHUNK, h, d), pad)
    vp = jnp.pad(v.reshape(b, nc, CHUNK, h, d), pad)
    kband = jnp.concatenate([kp[:, w:w + nc] for w in range(LEFT_CHUNKS + 1)], axis=2)
    vband = jnp.concatenate([vp[:, w:w + nc] for w in range(LEFT_CHUNKS + 1)], axis=2)
    qi = jnp.arange(CHUNK)
    kk = jnp.arange(BAND)
    rel = jnp.clip(qi[:, None] + LEFT_CHUNKS * CHUNK - kk[None, :], -REL_CLIP, REL_CLIP) + REL_CLIP
    bias = rel_bias[:, rel].astype(jnp.float32)
    valid = (jnp.arange(nc)[:, None] - LEFT_CHUNKS + kk[None, :] // CHUNK) >= 0
    scores = jnp.einsum('bnqhd,bnkhd->bnhqk', qc, kband).astype(jnp.float32) * (d ** -0.5)
    scores = scores + bias[None, None]
    scores = jnp.where(valid[None, :, None, None, :], scores, NEG_INF)
    p = jax.nn.softmax(scores, axis=-1).astype(v.dtype)
    o = jnp.einsum('bnhqk,bnkhd->bnqhd', p, vband)
    return o.reshape(b, s, h, d)


def diff_attention(q, k, v, lam, subln_g, lambda_init):
    b, s, h, _, dq = q.shape
    scale = dq ** -0.5
    slopes = jnp.exp2(-8.0 * jnp.arange(1, h + 1, dtype=jnp.float32) / h)
    outs = []
    for blk in range(s // Q_BLOCK):
        q0 = blk * Q_BLOCK
        kend = q0 + Q_BLOCK
        tpos = q0 + jnp.arange(Q_BLOCK)
        spos = jnp.arange(kend)
        allowed = (spos[None, :] // CHUNK) <= (tpos[:, None] // CHUNK)
        dist = jnp.abs(tpos[:, None] - spos[None, :]).astype(jnp.float32)
        alibi = -slopes[:, None, None] * dist[None]
        sc = jnp.einsum('bqhmd,bkhmd->bhmqk', q[:, q0:kend], k[:, :kend]).astype(jnp.float32) * scale
        sc = sc + alibi[None, :, None]
        sc = jnp.where(allowed[None, None, None], sc, NEG_INF)
        p = jax.nn.softmax(sc, axis=-1)
        w = p[:, :, 0] - lam * p[:, :, 1]
        outs.append(jnp.einsum('bhqk,bkhd->bqhd', w.astype(v.dtype), v[:, :kend]))
    o = jnp.concatenate(outs, axis=1)
    return rmsnorm(o, subln_g) * (1.0 - lambda_init)


def stick_breaking_attention(q, k, v):
    b, s, h, d = q.shape
    scale = d ** -0.5
    outs = []
    for blk in range(s // Q_BLOCK):
        q0 = blk * Q_BLOCK
        kend = q0 + Q_BLOCK
        tpos = q0 + jnp.arange(Q_BLOCK)
        spos = jnp.arange(kend)
        strict = spos[None, :] < tpos[:, None]
        z = jnp.einsum('bqhd,bkhd->bhqk', q[:, q0:kend], k[:, :kend]).astype(jnp.float32) * scale
        log_beta = jax.nn.log_sigmoid(z)
        log_one_minus = jnp.where(strict, jax.nn.log_sigmoid(-z), 0.0)
        suffix = lax.cumsum(log_one_minus, axis=3, reverse=True) - log_one_minus
        a = jnp.where(strict, jnp.exp(log_beta + suffix), 0.0)
        outs.append(jnp.einsum('bhqk,bkhd->bqhd', a.astype(v.dtype), v[:, :kend]))
    return jnp.concatenate(outs, axis=1)


def token_mixing(h, w_in, rel_bias, lq1, lk1, lq2, lk2, subln_g, w_out, lambda_init):
    b, s, _ = h.shape
    proj = h @ w_in
    sizes = [WIDTH_A, WIDTH_A, WIDTH_A, B_QK_WIDTH, B_QK_WIDTH, WIDTH_B, WIDTH_C, WIDTH_C, WIDTH_C]
    qa, ka, va, qb, kb, vb, qc, kc, vc = jnp.split(proj, [int(c) for c in np.cumsum(sizes)[:-1]], axis=-1)
    hd_a = (b, s, N_HEADS_A, HEAD_DIM)
    o_a = chunk_band_attention(qa.reshape(hd_a), ka.reshape(hd_a), va.reshape(hd_a), rel_bias)
    lam = (jnp.exp(jnp.sum(lq1.astype(jnp.float32) * lk1.astype(jnp.float32)))
           - jnp.exp(jnp.sum(lq2.astype(jnp.float32) * lk2.astype(jnp.float32))) + lambda_init)
    hd_bqk = (b, s, N_HEADS_B, 2, DIFF_QK_DIM)
    o_b = diff_attention(qb.reshape(hd_bqk), kb.reshape(hd_bqk), vb.reshape(b, s, N_HEADS_B, DIFF_V_DIM),
                         lam, subln_g, lambda_init)
    hd_c = (b, s, N_HEADS_C, HEAD_DIM)
    o_c = stick_breaking_attention(qc.reshape(hd_c), kc.reshape(hd_c), vc.reshape(hd_c))
    y = jnp.concatenate([o_a.reshape(b, s, WIDTH_A), o_b.reshape(b, s, WIDTH_B),
                         o_c.reshape(b, s, WIDTH_C)], axis=-1)
    return y @ w_out


def setup_inputs(seed: int = 0) -> dict:
    key = jax.random.key(seed)
    ks = jax.random.split(key, 32)
    f32 = jnp.float32

    def nrm(k, shape, scale):
        return jax.random.normal(k, shape, f32) * scale

    def gain(k, shape):
        return 1.0 + 0.05 * jax.random.normal(k, shape, f32)

    L = DEPTH
    return {
        "x": jax.random.normal(ks[0], (BATCH, SEQ, D_MODEL), f32),
        "ffn1_pre_g": gain(ks[1], (L, D_MODEL)),
        "ffn1_w_gate": nrm(ks[2], (L, D_MODEL, D_FF), D_MODEL ** -0.5),
        "ffn1_w_up": nrm(ks[3], (L, D_MODEL, D_FF), D_MODEL ** -0.5),
        "ffn1_w_down": nrm(ks[4], (L, D_FF, D_MODEL), D_FF ** -0.5),
        "ffn1_post_g": gain(ks[5], (L, D_MODEL)),
        "mix_pre_g": gain(ks[6], (L, D_MODEL)),
        "w_in": nrm(ks[7], (L, D_MODEL, IN_COLS), D_MODEL ** -0.5),
        "rel_bias": nrm(ks[8], (L, N_HEADS_A, 2 * REL_CLIP + 1), 0.2),
        "diff_lambda_q1": nrm(ks[9], (L, DIFF_QK_DIM), 0.1),
        "diff_lambda_k1": nrm(ks[10], (L, DIFF_QK_DIM), 0.1),
        "diff_lambda_q2": nrm(ks[11], (L, DIFF_QK_DIM), 0.1),
        "diff_lambda_k2": nrm(ks[12], (L, DIFF_QK_DIM), 0.1),
        "diff_subln_g": gain(ks[13], (L, DIFF_V_DIM)),
        "w_out": nrm(ks[14], (L, D_MIX, D_MODEL), D_MIX ** -0.5),
        "mix_post_g": gain(ks[15], (L, D_MODEL)),
        "ffn2_pre_g": gain(ks[16], (L, D_MODEL)),
        "ffn2_w_gate": nrm(ks[17], (L, D_MODEL, D_FF), D_MODEL ** -0.5),
        "ffn2_w_up": nrm(ks[18], (L, D_MODEL, D_FF), D_MODEL ** -0.5),
        "ffn2_w_down": nrm(ks[19], (L, D_FF, D_MODEL), D_FF ** -0.5),
        "ffn2_post_g": gain(ks[20], (L, D_MODEL)),
    }


def reference(x, ffn1_pre_g, ffn1_w_gate, ffn1_w_up, ffn1_w_down, ffn1_post_g,
              mix_pre_g, w_in, rel_bias, diff_lambda_q1, diff_lambda_k1, diff_lambda_q2,
              diff_lambda_k2, diff_subln_g, w_out, mix_post_g,
              ffn2_pre_g, ffn2_w_gate, ffn2_w_up, ffn2_w_down, ffn2_post_g):
    for l in range(DEPTH):
        lambda_init = 0.8 - 0.6 * math.exp(-0.3 * l)
        f = swiglu(rmsnorm(x, ffn1_pre_g[l]), ffn1_w_gate[l], ffn1_w_up[l], ffn1_w_down[l])
        x = x + FFN_RESIDUAL_WEIGHT * rmsnorm(f, ffn1_post_g[l])
        m = token_mixing(rmsnorm(x, mix_pre_g[l]), w_in[l], rel_bias[l],
                         diff_lambda_q1[l], diff_lambda_k1[l], diff_lambda_q2[l], diff_lambda_k2[l],
                         diff_subln_g[l], w_out[l], lambda_init)
        x = x + rmsnorm(m, mix_post_g[l])
        f = swiglu(rmsnorm(x, ffn2_pre_g[l]), ffn2_w_gate[l], ffn2_w_up[l], ffn2_w_down[l])
        x = x + FFN_RESIDUAL_WEIGHT * rmsnorm(f, ffn2_post_g[l])
    return x
```

```python
import functools
import math

import numpy as np
import jax
import jax.numpy as jnp
from jax import lax
from jax.experimental import pallas as pl
from jax.experimental.pallas import tpu as pltpu

D_MODEL = 1024
SEQ = 2048
DEPTH = 2
CHUNK = 64
HEAD_DIM = 64
N_HEADS_A = 6
N_HEADS_B = 4
N_HEADS_C = 6
DIFF_QK_DIM = 32
WIDTH_A = N_HEADS_A * HEAD_DIM
WIDTH_B = N_HEADS_B * HEAD_DIM
WIDTH_C = N_HEADS_C * HEAD_DIM
IN_COLS = 3 * WIDTH_A + 3 * WIDTH_B + 3 * WIDTH_C
LEFT_CHUNKS = 8
REL_CLIP = 128
D_FF = 2816
NORM_EPS = 1e-6
NEG_INF = -1e30
FFN_RESIDUAL_WEIGHT = 0.5

LANES = 128
QA_BLK, KA_BLK, VA_BLK = 0, 3, 6
QB_BLK, KB_BLK, VB_BLK = 9, 11, 13
QC_BLK, KC_BLK, VC_BLK = 15, 18, 21

TM = 512
FF_CHUNK = 256
TQ = 128
TK = 256
BAND_PAD = LEFT_CHUNKS * CHUNK
BAND_WIN = BAND_PAD + TQ
REL_PAD = 384
ROLL_W = 768
VMEM_LIMIT = 52 * 1024 * 1024

_F32 = jnp.float32
_BF16 = jnp.bfloat16
_NT = (((1,), (1,)), ((), ()))


def _rms(x, g):
    ms = jnp.mean(x * x, axis=-1, keepdims=True)
    return x * lax.rsqrt(ms + NORM_EPS) * g


def _ffn_body(x_ref, pre_g_ref, wg_ref, wu_ref, wd_ref, post_g_ref, o_ref, acc_ref):
    x = x_ref[...]
    h = _rms(x, pre_g_ref[...]).astype(_BF16)
    for c in range(D_FF // FF_CHUNK):
        sl = slice(c * FF_CHUNK, (c + 1) * FF_CHUNK)
        g = jnp.dot(h, wg_ref[:, sl], preferred_element_type=_F32)
        u = jnp.dot(h, wu_ref[:, sl], preferred_element_type=_F32)
        a = (g * jax.nn.sigmoid(g) * u).astype(_BF16)
        d = jnp.dot(a, wd_ref[sl, :], preferred_element_type=_F32)
        if c == 0:
            acc_ref[...] = d
        else:
            acc_ref[...] += d
    o_ref[...] = x + FFN_RESIDUAL_WEIGHT * _rms(acc_ref[...], post_g_ref[...])


def _resident(shape):
    return pl.BlockSpec(shape, lambda *_: (0,) * len(shape), pipeline_mode=pl.Buffered(1))


def _ffn(x, pre_g, wg, wu, wd, post_g):
    n = x.shape[0]
    return pl.pallas_call(
        _ffn_body,
        out_shape=jax.ShapeDtypeStruct((n, D_MODEL), _F32),
        grid=(n // TM,),
        in_specs=[
            pl.BlockSpec((TM, D_MODEL), lambda i: (i, 0)),
            _resident((1, D_MODEL)),
            _resident((D_MODEL, D_FF)),
            _resident((D_MODEL, D_FF)),
            _resident((D_FF, D_MODEL)),
            _resident((1, D_MODEL)),
        ],
        out_specs=pl.BlockSpec((TM, D_MODEL), lambda i: (i, 0)),
        scratch_shapes=[pltpu.VMEM((TM, D_MODEL), _F32)],
        compiler_params=pltpu.CompilerParams(
            dimension_semantics=("arbitrary",), vmem_limit_bytes=VMEM_LIMIT),
        name="ffn",
    )(x, pre_g, wg, wu, wd, post_g)


def _inproj_body(x_ref, g_ref, w_ref, scale_ref, o_ref):
    h = _rms(x_ref[...], g_ref[...]).astype(_BF16)
    cw = 512
    for c in range(IN_COLS // cw):
        sl = slice(c * cw, (c + 1) * cw)
        p = jnp.dot(h, w_ref[:, sl], preferred_element_type=_F32)
        o_ref[:, sl] = (p * scale_ref[:, sl]).astype(_BF16)


def _inproj(x, g, w, scale):
    n = x.shape[0]
    return pl.pallas_call(
        _inproj_body,
        out_shape=jax.ShapeDtypeStruct((n, IN_COLS), _BF16),
        grid=(n // TM,),
        in_specs=[
            pl.BlockSpec((TM, D_MODEL), lambda i: (i, 0)),
            _resident((1, D_MODEL)),
            _resident((D_MODEL, IN_COLS)),
            _resident((1, IN_COLS)),
        ],
        out_specs=pl.BlockSpec((TM, IN_COLS), lambda i: (i, 0)),
        compiler_params=pltpu.CompilerParams(
            dimension_semantics=("arbitrary",), vmem_limit_bytes=VMEM_LIMIT),
        name="inproj",
    )(x, g, w, scale)


def _attn_a_body(rb_ref, q_ref, k_ref, v_ref, o_ref, kpad, vpad, bias, rext):
    hp = pl.program_id(0)
    b = pl.program_id(1)
    qi = pl.program_id(2)

    @pl.when((b == 0) & (qi == 0))
    def _build_bias():
        m_i = lax.broadcasted_iota(jnp.int32, (REL_PAD, ROLL_W), 0)
        j_i = lax.broadcasted_iota(jnp.int32, (REL_PAD, ROLL_W), 1)
        idx = jnp.where(j_i < BAND_WIN,
                        jnp.clip(BAND_PAD - j_i, -REL_CLIP, REL_CLIP) + REL_CLIP,
                        2 * REL_CLIP)
        sel = jnp.where(m_i == idx, 1.0, 0.0).astype(_BF16)
        rb = rb_ref[...]
        hi = rb.astype(_BF16)
        r1 = rb - hi.astype(_F32)
        mid = r1.astype(_BF16)
        lo = (r1 - mid.astype(_F32)).astype(_BF16)
        rext[...] = (jnp.dot(hi, sel, preferred_element_type=_F32)
                     + jnp.dot(mid, sel, preferred_element_type=_F32)
                     + jnp.dot(lo, sel, preferred_element_type=_F32))
        qc = lax.broadcasted_iota(jnp.int32, (TQ, BAND_WIN), 0) // CHUNK
        kc = lax.broadcasted_iota(jnp.int32, (TQ, BAND_WIN), 1) // CHUNK
        d = kc - qc
        in_band = (d >= 0) & (d <= LEFT_CHUNKS)
        for h in range(2):
            row = rext[pl.ds(2 * hp + h, 1), :]
            full = jnp.broadcast_to(row, (TQ, ROLL_W))
            toep = pltpu.roll(full, 0, 1, stride=1, stride_axis=0)
            bias[h] = jnp.where(in_band, toep[:, :BAND_WIN], NEG_INF)

    @pl.when(qi == 0)
    def _stage_keys():
        zeros = jnp.zeros((BAND_PAD, LANES), _BF16)
        kpad[0:BAND_PAD, :] = zeros
        vpad[0:BAND_PAD, :] = zeros
        kpad[BAND_PAD:, :] = k_ref[0]
        vpad[BAND_PAD:, :] = v_ref[0]

    t0 = pl.multiple_of(qi * TQ, TQ)
    kwin = kpad[pl.ds(t0, BAND_WIN), :]
    vwin = vpad[pl.ds(t0, BAND_WIN), :]
    q = q_ref[0]
    lane = lax.broadcasted_iota(jnp.int32, (TQ, LANES), 1)
    col = lax.broadcasted_iota(jnp.int32, (TQ, BAND_WIN), 1)
    key_exists = col >= (BAND_PAD - t0)
    outs = []
    for h in range(2):
        own_lanes = (lane >= HEAD_DIM) if h else (lane < HEAD_DIM)
        qh = jnp.where(own_lanes, q, jnp.zeros_like(q))
        s = lax.dot_general(qh, kwin, _NT, preferred_element_type=_F32)
        s = jnp.where(key_exists, s + bias[h], NEG_INF)
        m = jnp.max(s, axis=-1, keepdims=True)
        p = jnp.exp(s - m)
        l = jnp.sum(p, axis=-1, keepdims=True)
        o = jnp.dot(p.astype(_BF16), vwin, preferred_element_type=_F32)
        outs.append(o / l)
    o_ref[0] = jnp.where(lane < HEAD_DIM, outs[0], outs[1]).astype(_BF16)


def _attn_a(proj, rb_pad):
    bsz = proj.shape[0]
    return pl.pallas_call(
        _attn_a_body,
        out_shape=jax.ShapeDtypeStruct((bsz, SEQ, WIDTH_A), _BF16),
        grid=(N_HEADS_A // 2, bsz, SEQ // TQ),
        in_specs=[
            pl.BlockSpec((8, REL_PAD), lambda hp, b, qi: (0, 0)),
            pl.BlockSpec((1, TQ, LANES), lambda hp, b, qi: (b, qi, QA_BLK + hp)),
            pl.BlockSpec((1, SEQ, LANES), lambda hp, b, qi: (b, 0, KA_BLK + hp)),
            pl.BlockSpec((1, SEQ, LANES), lambda hp, b, qi: (b, 0, VA_BLK + hp)),
        ],
        out_specs=pl.BlockSpec((1, TQ, LANES), lambda hp, b, qi: (b, qi, hp)),
        scratch_shapes=[
            pltpu.VMEM((SEQ + BAND_PAD, LANES), _BF16),
            pltpu.VMEM((SEQ + BAND_PAD, LANES), _BF16),
            pltpu.VMEM((2, TQ, BAND_WIN), _F32),
            pltpu.VMEM((8, ROLL_W), _F32),
        ],
        compiler_params=pltpu.CompilerParams(
            dimension_semantics=("arbitrary", "arbitrary", "arbitrary")),
        name="attn_a",
    )(rb_pad, proj, proj, proj)


def _attn_b_body(lq1_ref, lk1_ref, lq2_ref, lk2_ref, g_ref, q_ref, k_ref, v_ref, o_ref,
                 *, lambda_init):
    hp = pl.program_id(0)
    qi = pl.program_id(2)
    t0 = qi * TQ
    nkb = (t0 + TQ + TK - 1) // TK

    lam = (jnp.exp(jnp.sum(lq1_ref[...] * lk1_ref[...], axis=-1, keepdims=True))
           - jnp.exp(jnp.sum(lq2_ref[...] * lk2_ref[...], axis=-1, keepdims=True))
           + lambda_init)

    q = q_ref[0]
    lane = lax.broadcasted_iota(jnp.int32, (TQ, LANES), 1)
    qmaps = [jnp.where((lane // DIFF_QK_DIM) == i, q, jnp.zeros_like(q)) for i in range(4)]
    slope0 = jnp.where(hp == 0, 2.0 ** -2, 2.0 ** -6).astype(_F32)
    slopes = [slope0, slope0 * 0.25]
    row_t = t0 + lax.broadcasted_iota(jnp.int32, (TQ, TK), 0)

    def body(j, carry):
        ms, ls, accs = carry
        s0 = pl.multiple_of(j * TK, TK)
        kb = k_ref[0, pl.ds(s0, TK), :]
        vb = v_ref[0, pl.ds(s0, TK), :]
        col_s = s0 + lax.broadcasted_iota(jnp.int32, (TQ, TK), 1)
        dist = jnp.abs(row_t - col_s).astype(_F32)
        allowed = (col_s // CHUNK) <= (row_t // CHUNK)
        new_m, new_l, new_acc = [], [], []
        for i in range(4):
            s = lax.dot_general(qmaps[i], kb, _NT, preferred_element_type=_F32)
            s = s - slopes[i // 2] * dist
            s = jnp.where(allowed, s, NEG_INF)
            m_new = jnp.maximum(ms[i], jnp.max(s, axis=-1, keepdims=True))
            alpha = jnp.exp(ms[i] - m_new)
            p = jnp.exp(s - m_new)
            new_m.append(m_new)
            new_l.append(alpha * ls[i] + jnp.sum(p, axis=-1, keepdims=True))
            new_acc.append(alpha * accs[i]
                           + jnp.dot(p.astype(_BF16), vb, preferred_element_type=_F32))
        return tuple(new_m), tuple(new_l), tuple(new_acc)

    init = (tuple(jnp.full((TQ, 1), NEG_INF, _F32) for _ in range(4)),
            tuple(jnp.zeros((TQ, 1), _F32) for _ in range(4)),
            tuple(jnp.zeros((TQ, LANES), _F32) for _ in range(4)))
    _, ls, accs = lax.fori_loop(0, nkb, body, init)

    heads = [accs[2 * h] / ls[2 * h] - lam * (accs[2 * h + 1] / ls[2 * h + 1]) for h in range(2)]
    lo_half = lane < HEAD_DIM
    o = jnp.where(lo_half, heads[0], heads[1])
    o2 = o * o
    ss_lo = jnp.sum(jnp.where(lo_half, o2, 0.0), axis=-1, keepdims=True)
    ss_hi = jnp.sum(jnp.where(lo_half, 0.0, o2), axis=-1, keepdims=True)
    ms = jnp.where(lo_half, ss_lo, ss_hi) * (1.0 / HEAD_DIM)
    y = o * lax.rsqrt(ms + NORM_EPS) * g_ref[...]
    o_ref[0] = (y * (1.0 - lambda_init)).astype(_BF16)


def _attn_b(proj, lq1, lk1, lq2, lk2, g2, lambda_init):
    bsz = proj.shape[0]
    small = lambda w: pl.BlockSpec((1, w), lambda hp, b, qi: (0, 0))
    return pl.pallas_call(
        functools.partial(_attn_b_body, lambda_init=lambda_init),
        out_shape=jax.ShapeDtypeStruct((bsz, SEQ, WIDTH_B), _BF16),
        grid=(N_HEADS_B // 2, bsz, SEQ // TQ),
        in_specs=[
            small(DIFF_QK_DIM), small(DIFF_QK_DIM), small(DIFF_QK_DIM), small(DIFF_QK_DIM),
            small(LANES),
            pl.BlockSpec((1, TQ, LANES), lambda hp, b, qi: (b, qi, QB_BLK + hp)),
            pl.BlockSpec((1, SEQ, LANES), lambda hp, b, qi: (b, 0, KB_BLK + hp)),
            pl.BlockSpec((1, SEQ, LANES), lambda hp, b, qi: (b, 0, VB_BLK + hp)),
        ],
        out_specs=pl.BlockSpec((1, TQ, LANES), lambda hp, b, qi: (b, qi, hp)),
        compiler_params=pltpu.CompilerParams(
            dimension_semantics=("arbitrary", "arbitrary", "arbitrary")),
        name="attn_b",
    )(lq1, lk1, lq2, lk2, g2, proj, proj, proj)


def _attn_c_body(q_ref, k_ref, v_ref, o_ref):
    qi = pl.program_id(2)
    t0 = qi * TQ
    nkb = (t0 + TQ + TK - 1) // TK

    q = q_ref[0]
    lane = lax.broadcasted_iota(jnp.int32, (TQ, LANES), 1)
    qh = [jnp.where(lane < HEAD_DIM, q, jnp.zeros_like(q)),
          jnp.where(lane >= HEAD_DIM, q, jnp.zeros_like(q))]
    row_t = t0 + lax.broadcasted_iota(jnp.int32, (TQ, TK), 0)
    tri = jnp.where(lax.broadcasted_iota(jnp.int32, (TK, TK), 0)
                    > lax.broadcasted_iota(jnp.int32, (TK, TK), 1), 1.0, 0.0).astype(_BF16)

    def body(i, carry):
        rs, accs = carry
        j = nkb - 1 - i
        s0 = pl.multiple_of(j * TK, TK)
        kb = k_ref[0, pl.ds(s0, TK), :]
        vb = v_ref[0, pl.ds(s0, TK), :]
        strict = (s0 + lax.broadcasted_iota(jnp.int32, (TQ, TK), 1)) < row_t
        new_r, new_acc = [], []
        for h in range(2):
            z = lax.dot_general(qh[h], kb, _NT, preferred_element_type=_F32)
            log_beta = jnp.minimum(z, 0.0) - jnp.log(1.0 + jnp.exp(-jnp.abs(z)))
            lom = jnp.where(strict, log_beta - z, 0.0)
            hi = lom.astype(_BF16)
            lo = (lom - hi.astype(_F32)).astype(_BF16)
            suffix = (jnp.dot(hi, tri, preferred_element_type=_F32)
                      + jnp.dot(lo, tri, preferred_element_type=_F32) + rs[h])
            a = jnp.where(strict, jnp.exp(log_beta + suffix), 0.0)
            new_acc.append(accs[h] + jnp.dot(a.astype(_BF16), vb, preferred_element_type=_F32))
            new_r.append(rs[h] + jnp.sum(lom, axis=-1, keepdims=True))
        return tuple(new_r), tuple(new_acc)

    init = (tuple(jnp.zeros((TQ, 1), _F32) for _ in range(2)),
            tuple(jnp.zeros((TQ, LANES), _F32) for _ in range(2)))
    _, accs = lax.fori_loop(0, nkb, body, init)
    o_ref[0] = jnp.where(lane < HEAD_DIM, accs[0], accs[1]).astype(_BF16)


def _attn_c(proj):
    bsz = proj.shape[0]
    return pl.pallas_call(
        _attn_c_body,
        out_shape=jax.ShapeDtypeStruct((bsz, SEQ, WIDTH_C), _BF16),
        grid=(N_HEADS_C // 2, bsz, SEQ // TQ),
        in_specs=[
            pl.BlockSpec((1, TQ, LANES), lambda hp, b, qi: (b, qi, QC_BLK + hp)),
            pl.BlockSpec((1, SEQ, LANES), lambda hp, b, qi: (b, 0, KC_BLK + hp)),
            pl.BlockSpec((1, SEQ, LANES), lambda hp, b, qi: (b, 0, VC_BLK + hp)),
        ],
        out_specs=pl.BlockSpec((1, TQ, LANES), lambda hp, b, qi: (b, qi, hp)),
        compiler_params=pltpu.CompilerParams(
            dimension_semantics=("arbitrary", "arbitrary", "arbitrary")),
        name="attn_c",
    )(proj, proj, proj)


def _outproj_body(x_ref, ya_ref, yb_ref, yc_ref, w_ref, g_ref, o_ref):
    y = jnp.concatenate([ya_ref[...], yb_ref[...], yc_ref[...]], axis=-1)
    m = jnp.dot(y, w_ref[...], preferred_element_type=_F32)
    o_ref[...] = x_ref[...] + _rms(m, g_ref[...])


def _outproj(x, ya, yb, yc, w, g):
    n = x.shape[0]
    return pl.pallas_call(
        _outproj_body,
        out_shape=jax.ShapeDtypeStruct((n, D_MODEL), _F32),
        grid=(n // TM,),
        in_specs=[
            pl.BlockSpec((TM, D_MODEL), lambda i: (i, 0)),
            pl.BlockSpec((TM, WIDTH_A), lambda i: (i, 0)),
            pl.BlockSpec((TM, WIDTH_B), lambda i: (i, 0)),
            pl.BlockSpec((TM, WIDTH_C), lambda i: (i, 0)),
            _resident((D_MODEL, D_MODEL)),
            _resident((1, D_MODEL)),
        ],
        out_specs=pl.BlockSpec((TM, D_MODEL), lambda i: (i, 0)),
        compiler_params=pltpu.CompilerParams(
            dimension_semantics=("arbitrary",), vmem_limit_bytes=VMEM_LIMIT),
        name="outproj",
    )(x, ya, yb, yc, w, g)


def _q_scale_row():
    s = np.ones((1, IN_COLS), np.float32)
    s[:, QA_BLK * LANES:QA_BLK * LANES + WIDTH_A] = HEAD_DIM ** -0.5
    s[:, QB_BLK * LANES:QB_BLK * LANES + WIDTH_B] = DIFF_QK_DIM ** -0.5
    s[:, QC_BLK * LANES:QC_BLK * LANES + WIDTH_C] = HEAD_DIM ** -0.5
    return jnp.asarray(s)


def kernel(x, ffn1_pre_g, ffn1_w_gate, ffn1_w_up, ffn1_w_down, ffn1_post_g, mix_pre_g, w_in, rel_bias, diff_lambda_q1, diff_lambda_k1, diff_lambda_q2, diff_lambda_k2, diff_subln_g, w_out, mix_post_g, ffn2_pre_g, ffn2_w_gate, ffn2_w_up, ffn2_w_down, ffn2_post_g):
    bsz, seq, d = x.shape
    n = bsz * seq
    xf = x.reshape(n, d)
    q_scale = _q_scale_row()
    row = lambda v: v.reshape(1, -1)
    for l in range(DEPTH):
        lambda_init = 0.8 - 0.6 * math.exp(-0.3 * l)
        xf = _ffn(xf, row(ffn1_pre_g[l]), ffn1_w_gate[l].astype(_BF16), ffn1_w_up[l].astype(_BF16),
                  ffn1_w_down[l].astype(_BF16), row(ffn1_post_g[l]))
        proj = _inproj(xf, row(mix_pre_g[l]), w_in[l].astype(_BF16), q_scale)
        proj = proj.reshape(bsz, seq, IN_COLS)
        rb_pad = jnp.pad(rel_bias[l], ((0, 8 - N_HEADS_A), (0, REL_PAD - (2 * REL_CLIP + 1))))
        ya = _attn_a(proj, rb_pad)
        yb = _attn_b(proj, row(diff_lambda_q1[l]), row(diff_lambda_k1[l]),
                     row(diff_lambda_q2[l]), row(diff_lambda_k2[l]),
                     row(jnp.tile(diff_subln_g[l], 2)), lambda_init)
        yc = _attn_c(proj)
        xf = _outproj(xf, ya.reshape(n, WIDTH_A), yb.reshape(n, WIDTH_B), yc.reshape(n, WIDTH_C),
                      w_out[l].astype(_BF16), row(mix_post_g[l]))
        xf = _ffn(xf, row(ffn2_pre_g[l]), ffn2_w_gate[l].astype(_BF16), ffn2_w_up[l].astype(_BF16),
                  ffn2_w_down[l].astype(_BF16), row(ffn2_post_g[l]))
    return xf.reshape(bsz, seq, d)
```

```python
import functools
import math

import numpy as np
import jax
import jax.numpy as jnp
from jax import lax
from jax.experimental import pallas as pl
from jax.experimental.pallas import tpu as pltpu

D_MODEL = 1024
SEQ = 2048
DEPTH = 2
CHUNK = 64
HEAD_DIM = 64
N_HEADS_A = 6
N_HEADS_B = 4
N_HEADS_C = 6
DIFF_QK_DIM = 32
WIDTH_A = N_HEADS_A * HEAD_DIM
WIDTH_B = N_HEADS_B * HEAD_DIM
WIDTH_C = N_HEADS_C * HEAD_DIM
IN_COLS = 3 * WIDTH_A + 3 * WIDTH_B + 3 * WIDTH_C
LEFT_CHUNKS = 8
REL_CLIP = 128
D_FF = 2816
NORM_EPS = 1e-6
NEG_INF = -1e30
FFN_RESIDUAL_WEIGHT = 0.5
LOG2E = math.log2(math.e)

LANES = 128
QA_COL, KA_COL, VA_COL = 0, WIDTH_A, 2 * WIDTH_A
QB_COL, KB_COL, VB_COL = 3 * WIDTH_A, 3 * WIDTH_A + WIDTH_B, 3 * WIDTH_A + 2 * WIDTH_B
QC_COL = 3 * WIDTH_A + 3 * WIDTH_B
KC_COL, VC_COL = QC_COL + WIDTH_C, QC_COL + 2 * WIDTH_C

TM = 512
FF_CHUNK = 256
TQ = 256
TK = 256
SPAN = 512
BAND_TQ = 128
BAND_PAD = LEFT_CHUNKS * CHUNK
BAND_WIN = BAND_PAD + BAND_TQ
REL_PAD = 384
ROLL_W = 768
VMEM_LIMIT = 52 * 1024 * 1024

_F32 = jnp.float32
_BF16 = jnp.bfloat16
_NT = (((1,), (1,)), ((), ()))


def _rms(x, g):
    ms = jnp.mean(x * x, axis=-1, keepdims=True)
    return x * lax.rsqrt(ms + NORM_EPS) * g


def _emit_skewed(stages, n_items):
    for step in range(n_items + len(stages) - 1):
        for k, stage in enumerate(stages):
            if 0 <= step - k < n_items:
                stage(step - k)


def _head_halves(q):
    lane = lax.broadcasted_iota(jnp.int32, q.shape, 1)
    zero = jnp.zeros_like(q)
    return jnp.where(lane < HEAD_DIM, q, zero), jnp.where(lane >= HEAD_DIM, q, zero)


def _ffn_body(x_ref, pre_g_ref, wg_ref, wu_ref, wd_ref, post_g_ref, o_ref, acc_ref):
    x = x_ref[...]
    h = _rms(x, pre_g_ref[...]).astype(_BF16)
    for c in range(D_FF // FF_CHUNK):
        sl = slice(c * FF_CHUNK, (c + 1) * FF_CHUNK)
        g = jnp.dot(h, wg_ref[:, sl], preferred_element_type=_F32)
        u = jnp.dot(h, wu_ref[:, sl], preferred_element_type=_F32)
        a = (g * jax.nn.sigmoid(g) * u).astype(_BF16)
        d = jnp.dot(a, wd_ref[sl, :], preferred_element_type=_F32)
        if c == 0:
            acc_ref[...] = d
        else:
            acc_ref[...] += d
    o_ref[...] = x + FFN_RESIDUAL_WEIGHT * _rms(acc_ref[...], post_g_ref[...])


def _resident(shape):
    return pl.BlockSpec(shape, lambda *_: (0,) * len(shape), pipeline_mode=pl.Buffered(1))


def _ffn(x, pre_g, wg, wu, wd, post_g):
    n = x.shape[0]
    return pl.pallas_call(
        _ffn_body,
        out_shape=jax.ShapeDtypeStruct((n, D_MODEL), _F32),
        grid=(n // TM,),
        in_specs=[
            pl.BlockSpec((TM, D_MODEL), lambda i: (i, 0)),
            _resident((1, D_MODEL)),
            _resident((D_MODEL, D_FF)),
            _resident((D_MODEL, D_FF)),
            _resident((D_FF, D_MODEL)),
            _resident((1, D_MODEL)),
        ],
        out_specs=pl.BlockSpec((TM, D_MODEL), lambda i: (i, 0)),
        scratch_shapes=[pltpu.VMEM((TM, D_MODEL), _F32)],
        compiler_params=pltpu.CompilerParams(
            dimension_semantics=("arbitrary",), vmem_limit_bytes=VMEM_LIMIT),
        name="ffn",
    )(x, pre_g, wg, wu, wd, post_g)


def _inproj_body(x_ref, g_ref, w_ref, scale_ref, o_ref):
    h = _rms(x_ref[...], g_ref[...]).astype(_BF16)
    cw = 512
    for c in range(IN_COLS // cw):
        sl = slice(c * cw, (c + 1) * cw)
        p = jnp.dot(h, w_ref[:, sl], preferred_element_type=_F32)
        o_ref[:, sl] = (p * scale_ref[:, sl]).astype(_BF16)


def _inproj(x, g, w, scale):
    n = x.shape[0]
    return pl.pallas_call(
        _inproj_body,
        out_shape=jax.ShapeDtypeStruct((n, IN_COLS), _BF16),
        grid=(n // TM,),
        in_specs=[
            pl.BlockSpec((TM, D_MODEL), lambda i: (i, 0)),
            _resident((1, D_MODEL)),
            _resident((D_MODEL, IN_COLS)),
            _resident((1, IN_COLS)),
        ],
        out_specs=pl.BlockSpec((TM, IN_COLS), lambda i: (i, 0)),
        compiler_params=pltpu.CompilerParams(
            dimension_semantics=("arbitrary",), vmem_limit_bytes=VMEM_LIMIT),
        name="inproj",
    )(x, g, w, scale)


def _group_specs(q_col, k_col, v_col, width):
    assert q_col % width == 0 and k_col % width == 0 and v_col % width == 0
    qb, kb, vb = q_col // width, k_col // width, v_col // width
    return [
        pl.BlockSpec((1, TQ, width), lambda b, qi: (b, qi, qb)),
        pl.BlockSpec((1, SEQ, width), lambda b, qi: (b, 0, kb)),
        pl.BlockSpec((1, SEQ, width), lambda b, qi: (b, 0, vb)),
    ]


def _attn_a_body(rb_ref, q_ref, k_ref, v_ref, o_ref, kpad, vpad, bias, rext):
    b = pl.program_id(0)
    qi = pl.program_id(1)

    @pl.when((b == 0) & (qi == 0))
    def _build_bias():
        m_i = lax.broadcasted_iota(jnp.int32, (REL_PAD, ROLL_W), 0)
        j_i = lax.broadcasted_iota(jnp.int32, (REL_PAD, ROLL_W), 1)
        idx = jnp.where(j_i < BAND_WIN,
                        jnp.clip(BAND_PAD - j_i, -REL_CLIP, REL_CLIP) + REL_CLIP,
                        2 * REL_CLIP)
        sel = jnp.where(m_i == idx, 1.0, 0.0).astype(_BF16)
        rb = rb_ref[...]
        hi = rb.astype(_BF16)
        r1 = rb - hi.astype(_F32)
        mid = r1.astype(_BF16)
        lo = (r1 - mid.astype(_F32)).astype(_BF16)
        rext[...] = LOG2E * (jnp.dot(hi, sel, preferred_element_type=_F32)
                             + jnp.dot(mid, sel, preferred_element_type=_F32)
                             + jnp.dot(lo, sel, preferred_element_type=_F32))
        qc = lax.broadcasted_iota(jnp.int32, (BAND_TQ, BAND_WIN), 0) // CHUNK
        kc = lax.broadcasted_iota(jnp.int32, (BAND_TQ, BAND_WIN), 1) // CHUNK
        d = kc - qc
        in_band = (d >= 0) & (d <= LEFT_CHUNKS)
        for h in range(N_HEADS_A):
            full = jnp.broadcast_to(rext[h:h + 1, :], (BAND_TQ, ROLL_W))
            toep = pltpu.roll(full, 0, 1, stride=1, stride_axis=0)
            bias[h] = jnp.where(in_band, toep[:, :BAND_WIN], NEG_INF)

    @pl.when(qi == 0)
    def _stage_keys():
        zeros = jnp.zeros((BAND_PAD, WIDTH_A), _BF16)
        kpad[0:BAND_PAD, :] = zeros
        vpad[0:BAND_PAD, :] = zeros
        kpad[BAND_PAD:, :] = k_ref[0]
        vpad[BAND_PAD:, :] = v_ref[0]

    lane = lax.broadcasted_iota(jnp.int32, (BAND_TQ, LANES), 1)
    col = lax.broadcasted_iota(jnp.int32, (1, BAND_WIN), 1)
    n_sub = TQ // BAND_TQ
    n_pairs = N_HEADS_A // 2
    t0s = [pl.multiple_of(qi * TQ + sub * BAND_TQ, BAND_TQ) for sub in range(n_sub)]
    pad_bias = [jnp.where(col >= BAND_PAD - t0, 0.0, NEG_INF) for t0 in t0s]
    st = [dict() for _ in range(n_sub * N_HEADS_A)]

    def unpack(i):
        sub, head = divmod(i, N_HEADS_A)
        pair = head // 2
        return sub, head, slice(pair * LANES, (pair + 1) * LANES)

    def scores(i):
        sub, head, cols = unpack(i)
        q = q_ref[0, sub * BAND_TQ:(sub + 1) * BAND_TQ, cols]
        kwin = kpad[pl.ds(t0s[sub], BAND_WIN), cols]
        st[i]["s"] = lax.dot_general(_head_halves(q)[head % 2], kwin, _NT,
                                     preferred_element_type=_F32)

    def softmax(i):
        sub, head, _ = unpack(i)
        s = st[i].pop("s") + bias[head] + pad_bias[sub]
        p = jnp.exp2(s - jnp.max(s, axis=-1, keepdims=True))
        st[i]["l"] = jnp.sum(p, axis=-1, keepdims=True)
        st[i]["p"] = p.astype(_BF16)

    def values(i):
        sub, head, cols = unpack(i)
        vwin = vpad[pl.ds(t0s[sub], BAND_WIN), cols]
        o = jnp.dot(st[i].pop("p"), vwin, preferred_element_type=_F32) / st[i].pop("l")
        if head % 2 == 0:
            st[i]["o"] = o
        else:
            rows = slice(sub * BAND_TQ, (sub + 1) * BAND_TQ)
            o_ref[0, rows, cols] = jnp.where(lane < HEAD_DIM, st[i - 1].pop("o"), o).astype(_BF16)

    _emit_skewed((scores, softmax, values), n_sub * N_HEADS_A)


def _attn_a(proj, rb_pad):
    bsz = proj.shape[0]
    return pl.pallas_call(
        _attn_a_body,
        out_shape=jax.ShapeDtypeStruct((bsz, SEQ, WIDTH_A), _BF16),
        grid=(bsz, SEQ // TQ),
        in_specs=[pl.BlockSpec((8, REL_PAD), lambda b, qi: (0, 0))]
        + _group_specs(QA_COL, KA_COL, VA_COL, WIDTH_A),
        out_specs=pl.BlockSpec((1, TQ, WIDTH_A), lambda b, qi: (b, qi, 0)),
        scratch_shapes=[
            pltpu.VMEM((SEQ + BAND_PAD, WIDTH_A), _BF16),
            pltpu.VMEM((SEQ + BAND_PAD, WIDTH_A), _BF16),
            pltpu.VMEM((N_HEADS_A, BAND_TQ, BAND_WIN), _F32),
            pltpu.VMEM((8, ROLL_W), _F32),
        ],
        compiler_params=pltpu.CompilerParams(
            dimension_semantics=("arbitrary", "arbitrary"), vmem_limit_bytes=VMEM_LIMIT),
        name="attn_a",
    )(rb_pad, proj, proj, proj)


def _attn_b_body(lq1_ref, lk1_ref, lq2_ref, lk2_ref, g_ref,
                 q0_ref, k0_ref, v0_ref, q1_ref, k1_ref, v1_ref, o_ref, *, lambda_init):
    q_refs, k_refs, v_refs = (q0_ref, q1_ref), (k0_ref, k1_ref), (v0_ref, v1_ref)
    qi = pl.program_id(1)
    t0 = qi * TQ
    n_left = t0 // SPAN

    lam = (jnp.exp(jnp.sum(lq1_ref[...] * lk1_ref[...], axis=-1, keepdims=True))
           - jnp.exp(jnp.sum(lq2_ref[...] * lk2_ref[...], axis=-1, keepdims=True))
           + lambda_init)

    n_pairs = N_HEADS_B // 2
    n_maps = 2 * N_HEADS_B
    lane = lax.broadcasted_iota(jnp.int32, (TQ, LANES), 1)
    qmaps = []
    for pair in range(n_pairs):
        q = q_refs[pair][0]
        qmaps.append([jnp.where((lane // DIFF_QK_DIM) == i, q, jnp.zeros_like(q))
                      for i in range(4)])
    slopes = [LOG2E * 2.0 ** (-2 * (h + 1)) for h in range(N_HEADS_B)]

    def span(s0, carry, on_diagonal):
        ms, ls, accs = carry
        s0 = pl.multiple_of(s0, SPAN)
        if on_diagonal:
            row_t = t0 + lax.broadcasted_iota(jnp.int32, (TQ, SPAN), 0)
            col_s = s0 + lax.broadcasted_iota(jnp.int32, (TQ, SPAN), 1)
            allowed = (col_s // CHUNK) <= (row_t // CHUNK)
            shifted = (row_t - jnp.abs(row_t - col_s)).astype(_F32)
        else:
            shifted = (s0 + lax.broadcasted_iota(jnp.int32, (1, SPAN), 1)).astype(_F32)
        kbs = [k_refs[pair][0, pl.ds(s0, SPAN), :] for pair in range(n_pairs)]
        vbs = [v_refs[pair][0, pl.ds(s0, SPAN), :] for pair in range(n_pairs)]
        st = [dict() for _ in range(n_maps)]

        def scores(i):
            st[i]["s"] = lax.dot_general(qmaps[i // 4][i % 4], kbs[i // 4], _NT,
                                         preferred_element_type=_F32)

        def softmax(i):
            s = st[i].pop("s") + slopes[i // 2] * shifted
            if on_diagonal:
                s = jnp.where(allowed, s, NEG_INF)
            m_new = jnp.maximum(ms[i], jnp.max(s, axis=-1, keepdims=True))
            alpha = jnp.exp2(ms[i] - m_new)
            p = jnp.exp2(s - m_new)
            st[i].update(m=m_new, alpha=alpha, p=p.astype(_BF16),
                         l=alpha * ls[i] + jnp.sum(p, axis=-1, keepdims=True))

        def values(i):
            st[i]["acc"] = st[i].pop("alpha") * accs[i] + jnp.dot(
                st[i].pop("p"), vbs[i // 4], preferred_element_type=_F32)

        _emit_skewed((scores, softmax, values), n_maps)
        return (tuple(s["m"] for s in st), tuple(s["l"] for s in st),
                tuple(s["acc"] for s in st))

    init = (tuple(jnp.full((TQ, 1), NEG_INF, _F32) for _ in range(n_maps)),
            tuple(jnp.zeros((TQ, 1), _F32) for _ in range(n_maps)),
            tuple(jnp.zeros((TQ, LANES), _F32) for _ in range(n_maps)))
    carry = lax.fori_loop(0, n_left, lambda j, c: span(j * SPAN, c, False), init)
    _, ls, accs = span(n_left * SPAN, carry, True)

    lo_half = lane < HEAD_DIM
    for pair in range(n_pairs):
        heads = []
        for h in range(2):
            i0 = 4 * pair + 2 * h
            heads.append(accs[i0] / ls[i0] - lam * (accs[i0 + 1] / ls[i0 + 1]))
        o = jnp.where(lo_half, heads[0], heads[1])
        o2 = o * o
        ss_lo = jnp.sum(jnp.where(lo_half, o2, 0.0), axis=-1, keepdims=True)
        ss_hi = jnp.sum(jnp.where(lo_half, 0.0, o2), axis=-1, keepdims=True)
        ms = jnp.where(lo_half, ss_lo, ss_hi) * (1.0 / HEAD_DIM)
        cols = slice(pair * LANES, (pair + 1) * LANES)
        y = o * lax.rsqrt(ms + NORM_EPS) * g_ref[:, cols]
        o_ref[0, :, cols] = (y * (1.0 - lambda_init)).astype(_BF16)


def _attn_b(proj, lq1, lk1, lq2, lk2, g4, lambda_init):
    bsz = proj.shape[0]
    small = lambda w: pl.BlockSpec((1, w), lambda b, qi: (0, 0))
    return pl.pallas_call(
        functools.partial(_attn_b_body, lambda_init=lambda_init),
        out_shape=jax.ShapeDtypeStruct((bsz, SEQ, WIDTH_B), _BF16),
        grid=(bsz, SEQ // TQ),
        in_specs=[small(DIFF_QK_DIM), small(DIFF_QK_DIM), small(DIFF_QK_DIM),
                  small(DIFF_QK_DIM), small(WIDTH_B)]
        + _group_specs(QB_COL, KB_COL, VB_COL, LANES)
        + _group_specs(QB_COL + LANES, KB_COL + LANES, VB_COL + LANES, LANES),
        out_specs=pl.BlockSpec((1, TQ, WIDTH_B), lambda b, qi: (b, qi, 0)),
        compiler_params=pltpu.CompilerParams(
            dimension_semantics=("arbitrary", "arbitrary"), vmem_limit_bytes=VMEM_LIMIT),
        name="attn_b",
    )(lq1, lk1, lq2, lk2, g4, *([proj] * 6))


def _attn_c_body(q_ref, k_ref, v_ref, o_ref):
    qi = pl.program_id(1)
    t0 = qi * TQ

    n_pairs = N_HEADS_C // 2
    lane = lax.broadcasted_iota(jnp.int32, (TQ, LANES), 1)
    qh = [_head_halves(q_ref[0, :, pair * LANES:(pair + 1) * LANES]) for pair in range(n_pairs)]
    tri_blk = jnp.where(lax.broadcasted_iota(jnp.int32, (TK, TK), 0)
                        > lax.broadcasted_iota(jnp.int32, (TK, TK), 1), 1.0, 0.0).astype(_BF16)
    tri = jnp.concatenate([tri_blk, tri_blk], axis=0)

    def block(s0, carry, on_diagonal):
        rs, accs = carry
        s0 = pl.multiple_of(s0, TK)
        if on_diagonal:
            strict = (lax.broadcasted_iota(jnp.int32, (TQ, TK), 1)
                      < lax.broadcasted_iota(jnp.int32, (TQ, TK), 0))
        kbs = [k_ref[0, pl.ds(s0, TK), p * LANES:(p + 1) * LANES] for p in range(n_pairs)]
        vbs = [v_ref[0, pl.ds(s0, TK), p * LANES:(p + 1) * LANES] for p in range(n_pairs)]
        st = [dict() for _ in range(N_HEADS_C)]

        def scores(h):
            st[h]["z"] = lax.dot_general(qh[h // 2][h % 2], kbs[h // 2], _NT,
                                         preferred_element_type=_F32)

        def logs(h):
            z = st[h].pop("z")
            log_beta = jnp.minimum(z, 0.0) - LOG2E * jnp.log(1.0 + jnp.exp2(-jnp.abs(z)))
            lom = log_beta - z
            if on_diagonal:
                lom = jnp.where(strict, lom, 0.0)
            hi = lom.astype(_BF16)
            lo = (lom - hi.astype(_F32)).astype(_BF16)
            st[h]["log_beta"] = log_beta
            st[h]["split"] = jnp.concatenate([hi, lo], axis=1)
            st[h]["r_new"] = rs[h] + jnp.sum(lom, axis=-1, keepdims=True)

        def suffix(h):
            st[h]["suffix"] = jnp.dot(st[h].pop("split"), tri, preferred_element_type=_F32)

        def weights(h):
            a = jnp.exp2(st[h].pop("log_beta") + st[h].pop("suffix") + rs[h])
            if on_diagonal:
                a = jnp.where(strict, a, 0.0)
            st[h]["a"] = a.astype(_BF16)

        def values(h):
            st[h]["pv"] = jnp.dot(st[h].pop("a"), vbs[h // 2], preferred_element_type=_F32)

        _emit_skewed((scores, logs, suffix, weights, values), N_HEADS_C)
        new_r = tuple(st[h]["r_new"] for h in range(N_HEADS_C))
        new_acc = tuple(
            accs[p] + jnp.where(lane < HEAD_DIM, st[2 * p]["pv"], st[2 * p + 1]["pv"])
            for p in range(n_pairs))
        return new_r, new_acc

    init = (tuple(jnp.zeros((TQ, 1), _F32) for _ in range(N_HEADS_C)),
            tuple(jnp.zeros((TQ, LANES), _F32) for _ in range(n_pairs)))
    carry = block(t0, init, True)
    _, accs = lax.fori_loop(0, qi, lambda i, c: block((qi - 1 - i) * TK, c, False), carry)
    for pair in range(n_pairs):
        o_ref[0, :, pair * LANES:(pair + 1) * LANES] = accs[pair].astype(_BF16)


def _attn_c(proj):
    bsz = proj.shape[0]
    return pl.pallas_call(
        _attn_c_body,
        out_shape=jax.ShapeDtypeStruct((bsz, SEQ, WIDTH_C), _BF16),
        grid=(bsz, SEQ // TQ),
        in_specs=_group_specs(QC_COL, KC_COL, VC_COL, WIDTH_C),
        out_specs=pl.BlockSpec((1, TQ, WIDTH_C), lambda b, qi: (b, qi, 0)),
        compiler_params=pltpu.CompilerParams(
            dimension_semantics=("arbitrary", "arbitrary"), vmem_limit_bytes=VMEM_LIMIT),
        name="attn_c",
    )(proj, proj, proj)


def _outproj_body(x_ref, ya_ref, yb_ref, yc_ref, w_ref, g_ref, o_ref):
    y = jnp.concatenate([ya_ref[...], yb_ref[...], yc_ref[...]], axis=-1)
    m = jnp.dot(y, w_ref[...], preferred_element_type=_F32)
    o_ref[...] = x_ref[...] + _rms(m, g_ref[...])


def _outproj(x, ya, yb, yc, w, g):
    n = x.shape[0]
    return pl.pallas_call(
        _outproj_body,
        out_shape=jax.ShapeDtypeStruct((n, D_MODEL), _F32),
        grid=(n // TM,),
        in_specs=[
            pl.BlockSpec((TM, D_MODEL), lambda i: (i, 0)),
            pl.BlockSpec((TM, WIDTH_A), lambda i: (i, 0)),
            pl.BlockSpec((TM, WIDTH_B), lambda i: (i, 0)),
            pl.BlockSpec((TM, WIDTH_C), lambda i: (i, 0)),
            _resident((D_MODEL, D_MODEL)),
            _resident((1, D_MODEL)),
        ],
        out_specs=pl.BlockSpec((TM, D_MODEL), lambda i: (i, 0)),
        compiler_params=pltpu.CompilerParams(
            dimension_semantics=("arbitrary",), vmem_limit_bytes=VMEM_LIMIT),
        name="outproj",
    )(x, ya, yb, yc, w, g)


def _q_scale_row():
    s = np.ones((1, IN_COLS), np.float32)
    s[:, QA_COL:QA_COL + WIDTH_A] = HEAD_DIM ** -0.5 * LOG2E
    s[:, QB_COL:QB_COL + WIDTH_B] = DIFF_QK_DIM ** -0.5 * LOG2E
    s[:, QC_COL:QC_COL + WIDTH_C] = HEAD_DIM ** -0.5 * LOG2E
    return jnp.asarray(s)


def kernel(x, ffn1_pre_g, ffn1_w_gate, ffn1_w_up, ffn1_w_down, ffn1_post_g, mix_pre_g, w_in, rel_bias, diff_lambda_q1, diff_lambda_k1, diff_lambda_q2, diff_lambda_k2, diff_subln_g, w_out, mix_post_g, ffn2_pre_g, ffn2_w_gate, ffn2_w_up, ffn2_w_down, ffn2_post_g):
    bsz, seq, d = x.shape
    n = bsz * seq
    xf = x.reshape(n, d)
    q_scale = _q_scale_row()
    row = lambda v: v.reshape(1, -1)
    for l in range(DEPTH):
        lambda_init = 0.8 - 0.6 * math.exp(-0.3 * l)
        xf = _ffn(xf, row(ffn1_pre_g[l]), ffn1_w_gate[l].astype(_BF16), ffn1_w_up[l].astype(_BF16),
                  ffn1_w_down[l].astype(_BF16), row(ffn1_post_g[l]))
        proj = _inproj(xf, row(mix_pre_g[l]), w_in[l].astype(_BF16), q_scale)
        proj = proj.reshape(bsz, seq, IN_COLS)
        rb_pad = jnp.pad(rel_bias[l], ((0, 8 - N_HEADS_A), (0, REL_PAD - (2 * REL_CLIP + 1))))
        ya = _attn_a(proj, rb_pad)
        yb = _attn_b(proj, row(diff_lambda_q1[l]), row(diff_lambda_k1[l]),
                     row(diff_lambda_q2[l]), row(diff_lambda_k2[l]),
                     row(jnp.tile(diff_subln_g[l], N_HEADS_B)), lambda_init)
        yc = _attn_c(proj)
        xf = _outproj(xf, ya.reshape(n, WIDTH_A), yb.reshape(n, WIDTH_B), yc.reshape(n, WIDTH_C),
                      w_out[l].astype(_BF16), row(mix_post_g[l]))
        xf = _ffn(xf, row(ffn2_pre_g[l]), ffn2_w_gate[l].astype(_BF16), ffn2_w_up[l].astype(_BF16),
                  ffn2_w_down[l].astype(_BF16), row(ffn2_post_g[l]))
    return xf.reshape(bsz, seq, d)
```

```python
import functools
import math

import numpy as np
import jax
import jax.numpy as jnp
from jax import lax
from jax.experimental import pallas as pl
from jax.experimental.pallas import tpu as pltpu

D_MODEL = 1024
SEQ = 2048
DEPTH = 2
CHUNK = 64
HEAD_DIM = 64
N_HEADS_A = 6
N_HEADS_B = 4
N_HEADS_C = 6
DIFF_QK_DIM = 32
WIDTH_A = N_HEADS_A * HEAD_DIM
WIDTH_B = N_HEADS_B * HEAD_DIM
WIDTH_C = N_HEADS_C * HEAD_DIM
IN_COLS = 3 * WIDTH_A + 3 * WIDTH_B + 3 * WIDTH_C
LEFT_CHUNKS = 8
REL_CLIP = 128
D_FF = 2816
NORM_EPS = 1e-6
NEG_INF = -1e30
FFN_RESIDUAL_WEIGHT = 0.5
LOG2E = math.log2(math.e)
F32_UNDERFLOW_LOG2 = -152.0

LANES = 128
QA_COL, KA_COL, VA_COL = 0, WIDTH_A, 2 * WIDTH_A
QB_COL, KB_COL, VB_COL = 3 * WIDTH_A, 3 * WIDTH_A + WIDTH_B, 3 * WIDTH_A + 2 * WIDTH_B
QC_COL = 3 * WIDTH_A + 3 * WIDTH_B
KC_COL, VC_COL = QC_COL + WIDTH_C, QC_COL + 2 * WIDTH_C

TM = 512
FF_CHUNK = 256
TQ = 256
TK = 256
SPAN = 512
BAND_TQ = 128
BAND_PAD = LEFT_CHUNKS * CHUNK
BAND_WIN = BAND_PAD + BAND_TQ
REL_PAD = 384
ROLL_W = 768
VMEM_LIMIT = 52 * 1024 * 1024

_F32 = jnp.float32
_BF16 = jnp.bfloat16
_NT = (((1,), (1,)), ((), ()))


def _rms(x, g):
    ms = jnp.mean(x * x, axis=-1, keepdims=True)
    return x * lax.rsqrt(ms + NORM_EPS) * g


def _emit_skewed(stages, n_items):
    for step in range(n_items + len(stages) - 1):
        for k, stage in enumerate(stages):
            if 0 <= step - k < n_items:
                stage(step - k)


def _head_halves(q):
    lane = lax.broadcasted_iota(jnp.int32, q.shape, 1)
    zero = jnp.zeros_like(q)
    return jnp.where(lane < HEAD_DIM, q, zero), jnp.where(lane >= HEAD_DIM, q, zero)


def _swiglu_half_step(x, pre_g_ref, wg_ref, wu_ref, wd_ref, post_g_ref, acc_ref):
    h = _rms(x, pre_g_ref[...]).astype(_BF16)
    for c in range(D_FF // FF_CHUNK):
        sl = slice(c * FF_CHUNK, (c + 1) * FF_CHUNK)
        g = jnp.dot(h, wg_ref[:, sl], preferred_element_type=_F32)
        u = jnp.dot(h, wu_ref[:, sl], preferred_element_type=_F32)
        a = (g * jax.nn.sigmoid(g) * u).astype(_BF16)
        d = jnp.dot(a, wd_ref[sl, :], preferred_element_type=_F32)
        if c == 0:
            acc_ref[...] = d
        else:
            acc_ref[...] += d
    return x + FFN_RESIDUAL_WEIGHT * _rms(acc_ref[...], post_g_ref[...])


def _resident(shape):
    return pl.BlockSpec(shape, lambda *_: (0,) * len(shape), pipeline_mode=pl.Buffered(1))


def _row_tile(width):
    return pl.BlockSpec((TM, width), lambda i: (i, 0))


_FFN_WEIGHT_SPECS = [_resident((1, D_MODEL)), _resident((D_MODEL, D_FF)),
                     _resident((D_MODEL, D_FF)), _resident((D_FF, D_MODEL)),
                     _resident((1, D_MODEL))]
_DENSE_PARAMS = pltpu.CompilerParams(dimension_semantics=("arbitrary",),
                                     vmem_limit_bytes=VMEM_LIMIT)


def _ffn_inproj_body(x_ref, pre_g_ref, wg_ref, wu_ref, wd_ref, post_g_ref,
                     mix_g_ref, w_in_ref, scale_ref, x_out_ref, proj_ref, acc_ref):
    x1 = _swiglu_half_step(x_ref[...], pre_g_ref, wg_ref, wu_ref, wd_ref, post_g_ref, acc_ref)
    x_out_ref[...] = x1
    h = _rms(x1, mix_g_ref[...]).astype(_BF16)
    cw = 512
    for c in range(IN_COLS // cw):
        sl = slice(c * cw, (c + 1) * cw)
        p = jnp.dot(h, w_in_ref[:, sl], preferred_element_type=_F32)
        proj_ref[:, sl] = (p * scale_ref[:, sl]).astype(_BF16)


def _ffn_inproj(x, ffn_weights, mix_g, w_in, scale):
    n = x.shape[0]
    return pl.pallas_call(
        _ffn_inproj_body,
        out_shape=(jax.ShapeDtypeStruct((n, D_MODEL), _F32),
                   jax.ShapeDtypeStruct((n, IN_COLS), _BF16)),
        grid=(n // TM,),
        in_specs=[_row_tile(D_MODEL)] + _FFN_WEIGHT_SPECS
        + [_resident((1, D_MODEL)), _resident((D_MODEL, IN_COLS)), _resident((1, IN_COLS))],
        out_specs=(_row_tile(D_MODEL), _row_tile(IN_COLS)),
        scratch_shapes=[pltpu.VMEM((TM, D_MODEL), _F32)],
        compiler_params=_DENSE_PARAMS,
        name="ffn_inproj",
    )(x, *ffn_weights, mix_g, w_in, scale)


def _outproj_ffn_body(x_ref, ya_ref, yb_ref, yc_ref, w_out_ref, mix_post_g_ref,
                      pre_g_ref, wg_ref, wu_ref, wd_ref, post_g_ref, o_ref, acc_ref):
    y = jnp.concatenate([ya_ref[...], yb_ref[...], yc_ref[...]], axis=-1)
    m = jnp.dot(y, w_out_ref[...], preferred_element_type=_F32)
    x1 = x_ref[...] + _rms(m, mix_post_g_ref[...])
    o_ref[...] = _swiglu_half_step(x1, pre_g_ref, wg_ref, wu_ref, wd_ref, post_g_ref, acc_ref)


def _outproj_ffn(x, ya, yb, yc, w_out, mix_post_g, ffn_weights):
    n = x.shape[0]
    return pl.pallas_call(
        _outproj_ffn_body,
        out_shape=jax.ShapeDtypeStruct((n, D_MODEL), _F32),
        grid=(n // TM,),
        in_specs=[_row_tile(D_MODEL), _row_tile(WIDTH_A), _row_tile(WIDTH_B), _row_tile(WIDTH_C),
                  _resident((D_MODEL, D_MODEL)), _resident((1, D_MODEL))] + _FFN_WEIGHT_SPECS,
        out_specs=_row_tile(D_MODEL),
        scratch_shapes=[pltpu.VMEM((TM, D_MODEL), _F32)],
        compiler_params=_DENSE_PARAMS,
        name="outproj_ffn",
    )(x, ya, yb, yc, w_out, mix_post_g, *ffn_weights)


def _group_specs(q_col, k_col, v_col, width):
    assert q_col % width == 0 and k_col % width == 0 and v_col % width == 0
    qb, kb, vb = q_col // width, k_col // width, v_col // width
    return [
        pl.BlockSpec((1, TQ, width), lambda b, qi: (b, qi, qb)),
        pl.BlockSpec((1, SEQ, width), lambda b, qi: (b, 0, kb)),
        pl.BlockSpec((1, SEQ, width), lambda b, qi: (b, 0, vb)),
    ]


def _attn_a_body(rb_ref, q_ref, k_ref, v_ref, o_ref, kpad, vpad, bias, rext):
    b = pl.program_id(0)
    qi = pl.program_id(1)

    @pl.when((b == 0) & (qi == 0))
    def _build_bias():
        m_i = lax.broadcasted_iota(jnp.int32, (REL_PAD, ROLL_W), 0)
        j_i = lax.broadcasted_iota(jnp.int32, (REL_PAD, ROLL_W), 1)
        idx = jnp.where(j_i < BAND_WIN,
                        jnp.clip(BAND_PAD - j_i, -REL_CLIP, REL_CLIP) + REL_CLIP,
                        2 * REL_CLIP)
        sel = jnp.where(m_i == idx, 1.0, 0.0).astype(_BF16)
        rb = rb_ref[...]
        hi = rb.astype(_BF16)
        r1 = rb - hi.astype(_F32)
        mid = r1.astype(_BF16)
        lo = (r1 - mid.astype(_F32)).astype(_BF16)
        rext[...] = LOG2E * (jnp.dot(hi, sel, preferred_element_type=_F32)
                             + jnp.dot(mid, sel, preferred_element_type=_F32)
                             + jnp.dot(lo, sel, preferred_element_type=_F32))
        qc = lax.broadcasted_iota(jnp.int32, (BAND_TQ, BAND_WIN), 0) // CHUNK
        kc = lax.broadcasted_iota(jnp.int32, (BAND_TQ, BAND_WIN), 1) // CHUNK
        d = kc - qc
        in_band = (d >= 0) & (d <= LEFT_CHUNKS)
        for h in range(N_HEADS_A):
            full = jnp.broadcast_to(rext[h:h + 1, :], (BAND_TQ, ROLL_W))
            toep = pltpu.roll(full, 0, 1, stride=1, stride_axis=0)
            bias[h] = jnp.where(in_band, toep[:, :BAND_WIN], NEG_INF)

    @pl.when(qi == 0)
    def _stage_keys():
        zeros = jnp.zeros((BAND_PAD, WIDTH_A), _BF16)
        kpad[0:BAND_PAD, :] = zeros
        vpad[0:BAND_PAD, :] = zeros
        kpad[BAND_PAD:, :] = k_ref[0]
        vpad[BAND_PAD:, :] = v_ref[0]

    lane = lax.broadcasted_iota(jnp.int32, (BAND_TQ, LANES), 1)
    col = lax.broadcasted_iota(jnp.int32, (1, BAND_WIN), 1)
    n_sub = TQ // BAND_TQ
    n_pairs = N_HEADS_A // 2
    t0s = [pl.multiple_of(qi * TQ + sub * BAND_TQ, BAND_TQ) for sub in range(n_sub)]
    pad_bias = [jnp.where(col >= BAND_PAD - t0, 0.0, NEG_INF) for t0 in t0s]
    st = [dict() for _ in range(n_sub * N_HEADS_A)]

    def unpack(i):
        sub, head = divmod(i, N_HEADS_A)
        pair = head // 2
        return sub, head, slice(pair * LANES, (pair + 1) * LANES)

    def scores(i):
        sub, head, cols = unpack(i)
        q = q_ref[0, sub * BAND_TQ:(sub + 1) * BAND_TQ, cols]
        kwin = kpad[pl.ds(t0s[sub], BAND_WIN), cols]
        st[i]["s"] = lax.dot_general(_head_halves(q)[head % 2], kwin, _NT,
                                     preferred_element_type=_F32)

    def softmax(i):
        sub, head, _ = unpack(i)
        s = st[i].pop("s") + bias[head] + pad_bias[sub]
        st[i]["p"] = jnp.exp2(s - jnp.max(s, axis=-1, keepdims=True)).astype(_BF16)

    ones = jnp.ones((BAND_WIN, LANES), _BF16)

    def values(i):
        sub, head, cols = unpack(i)
        vwin = jnp.concatenate([vpad[pl.ds(t0s[sub], BAND_WIN), cols], ones], axis=1)
        o = jnp.dot(st[i].pop("p"), vwin, preferred_element_type=_F32)
        o = o[:, :LANES] / o[:, LANES:]
        if head % 2 == 0:
            st[i]["o"] = o
        else:
            rows = slice(sub * BAND_TQ, (sub + 1) * BAND_TQ)
            o_ref[0, rows, cols] = jnp.where(lane < HEAD_DIM, st[i - 1].pop("o"), o).astype(_BF16)

    _emit_skewed((scores, softmax, values), n_sub * N_HEADS_A)


def _attn_a(proj, rb_pad):
    bsz = proj.shape[0]
    return pl.pallas_call(
        _attn_a_body,
        out_shape=jax.ShapeDtypeStruct((bsz, SEQ, WIDTH_A), _BF16),
        grid=(bsz, SEQ // TQ),
        in_specs=[pl.BlockSpec((8, REL_PAD), lambda b, qi: (0, 0))]
        + _group_specs(QA_COL, KA_COL, VA_COL, WIDTH_A),
        out_specs=pl.BlockSpec((1, TQ, WIDTH_A), lambda b, qi: (b, qi, 0)),
        scratch_shapes=[
            pltpu.VMEM((SEQ + BAND_PAD, WIDTH_A), _BF16),
            pltpu.VMEM((SEQ + BAND_PAD, WIDTH_A), _BF16),
            pltpu.VMEM((N_HEADS_A, BAND_TQ, BAND_WIN), _F32),
            pltpu.VMEM((8, ROLL_W), _F32),
        ],
        compiler_params=pltpu.CompilerParams(
            dimension_semantics=("arbitrary", "arbitrary"), vmem_limit_bytes=VMEM_LIMIT),
        name="attn_a",
    )(rb_pad, proj, proj, proj)


def _attn_b_body(lq1_ref, lk1_ref, lq2_ref, lk2_ref, g_ref,
                 q0_ref, k0_ref, v0_ref, q1_ref, k1_ref, v1_ref, o_ref, *, lambda_init):
    q_refs, k_refs, v_refs = (q0_ref, q1_ref), (k0_ref, k1_ref), (v0_ref, v1_ref)
    qi = pl.program_id(1)
    t0 = qi * TQ
    n_left = t0 // SPAN

    lam = (jnp.exp(jnp.sum(lq1_ref[...] * lk1_ref[...], axis=-1, keepdims=True))
           - jnp.exp(jnp.sum(lq2_ref[...] * lk2_ref[...], axis=-1, keepdims=True))
           + lambda_init)

    n_pairs = N_HEADS_B // 2
    n_maps = 2 * N_HEADS_B
    lane = lax.broadcasted_iota(jnp.int32, (TQ, LANES), 1)
    qmaps = []
    for pair in range(n_pairs):
        q = q_refs[pair][0]
        qmaps.append([jnp.where((lane // DIFF_QK_DIM) == i, q, jnp.zeros_like(q))
                      for i in range(4)])
    slopes = [LOG2E * 2.0 ** (-2 * (h + 1)) for h in range(N_HEADS_B)]

    def span(s0, carry, width, on_diagonal):
        ms, accs = carry
        s0 = pl.multiple_of(s0, width)
        if on_diagonal:
            row_t = t0 + lax.broadcasted_iota(jnp.int32, (TQ, width), 0)
            col_s = s0 + lax.broadcasted_iota(jnp.int32, (TQ, width), 1)
            allowed = (col_s // CHUNK) <= (row_t // CHUNK)
            shifted = (row_t - jnp.abs(row_t - col_s)).astype(_F32)
        else:
            shifted = (s0 + lax.broadcasted_iota(jnp.int32, (1, width), 1)).astype(_F32)
        kbs = [k_refs[pair][0, pl.ds(s0, width), :] for pair in range(n_pairs)]
        ones = jnp.ones((width, LANES), _BF16)
        vbs = [jnp.concatenate([v_refs[pair][0, pl.ds(s0, width), :], ones], axis=1)
               for pair in range(n_pairs)]
        st = [dict() for _ in range(n_maps)]

        def scores(i):
            st[i]["s"] = lax.dot_general(qmaps[i // 4][i % 4], kbs[i // 4], _NT,
                                         preferred_element_type=_F32)

        def softmax(i):
            s = st[i].pop("s") + slopes[i // 2] * shifted
            if on_diagonal:
                s = jnp.where(allowed, s, NEG_INF)
            m_new = jnp.maximum(ms[i], jnp.max(s, axis=-1, keepdims=True))
            alpha = jnp.exp2(ms[i] - m_new)
            p = jnp.exp2(s - m_new)
            st[i].update(m=m_new, alpha=alpha, p=p.astype(_BF16))

        def values(i):
            st[i]["acc"] = st[i].pop("alpha") * accs[i] + jnp.dot(
                st[i].pop("p"), vbs[i // 4], preferred_element_type=_F32)

        _emit_skewed((scores, softmax, values), n_maps)
        return tuple(s["m"] for s in st), tuple(s["acc"] for s in st)

    init = (tuple(jnp.full((TQ, 1), NEG_INF, _F32) for _ in range(n_maps)),
            tuple(jnp.zeros((TQ, 2 * LANES), _F32) for _ in range(n_maps)))
    carry = lax.fori_loop(0, n_left, lambda j, c: span(j * SPAN, c, SPAN, False), init)
    _, accs = span(n_left * SPAN, carry, SPAN, True)
    outs = [acc[:, :LANES] / acc[:, LANES:] for acc in accs]

    lo_half = lane < HEAD_DIM
    for pair in range(n_pairs):
        heads = []
        for h in range(2):
            i0 = 4 * pair + 2 * h
            heads.append(outs[i0] - lam * outs[i0 + 1])
        o = jnp.where(lo_half, heads[0], heads[1])
        o2 = o * o
        ss_lo = jnp.sum(jnp.where(lo_half, o2, 0.0), axis=-1, keepdims=True)
        ss_hi = jnp.sum(jnp.where(lo_half, 0.0, o2), axis=-1, keepdims=True)
        ms = jnp.where(lo_half, ss_lo, ss_hi) * (1.0 / HEAD_DIM)
        cols = slice(pair * LANES, (pair + 1) * LANES)
        y = o * lax.rsqrt(ms + NORM_EPS) * g_ref[:, cols]
        o_ref[0, :, cols] = (y * (1.0 - lambda_init)).astype(_BF16)


def _attn_b(proj, lq1, lk1, lq2, lk2, g4, lambda_init):
    bsz = proj.shape[0]
    small = lambda w: pl.BlockSpec((1, w), lambda b, qi: (0, 0))
    return pl.pallas_call(
        functools.partial(_attn_b_body, lambda_init=lambda_init),
        out_shape=jax.ShapeDtypeStruct((bsz, SEQ, WIDTH_B), _BF16),
        grid=(bsz, SEQ // TQ),
        in_specs=[small(DIFF_QK_DIM), small(DIFF_QK_DIM), small(DIFF_QK_DIM),
                  small(DIFF_QK_DIM), small(WIDTH_B)]
        + _group_specs(QB_COL, KB_COL, VB_COL, LANES)
        + _group_specs(QB_COL + LANES, KB_COL + LANES, VB_COL + LANES, LANES),
        out_specs=pl.BlockSpec((1, TQ, WIDTH_B), lambda b, qi: (b, qi, 0)),
        compiler_params=pltpu.CompilerParams(
            dimension_semantics=("arbitrary", "arbitrary"), vmem_limit_bytes=VMEM_LIMIT),
        name="attn_b",
    )(lq1, lk1, lq2, lk2, g4, *([proj] * 6))


def _attn_c_body(q_ref, k_ref, v_ref, o_ref):
    qi = pl.program_id(1)
    t0 = qi * TQ

    n_pairs = N_HEADS_C // 2
    lane = lax.broadcasted_iota(jnp.int32, (TQ, LANES), 1)
    qh = [_head_halves(q_ref[0, :, pair * LANES:(pair + 1) * LANES]) for pair in range(n_pairs)]
    tri_blk = jnp.where(lax.broadcasted_iota(jnp.int32, (TK, TK), 0)
                        > lax.broadcasted_iota(jnp.int32, (TK, TK), 1), 1.0, 0.0).astype(_BF16)
    tri = jnp.concatenate([tri_blk, tri_blk], axis=0)

    def block(s0, carry, on_diagonal):
        rs, accs = carry
        s0 = pl.multiple_of(s0, TK)
        if on_diagonal:
            strict = (lax.broadcasted_iota(jnp.int32, (TQ, TK), 1)
                      < lax.broadcasted_iota(jnp.int32, (TQ, TK), 0))
        kbs = [k_ref[0, pl.ds(s0, TK), p * LANES:(p + 1) * LANES] for p in range(n_pairs)]
        vbs = [v_ref[0, pl.ds(s0, TK), p * LANES:(p + 1) * LANES] for p in range(n_pairs)]
        st = [dict() for _ in range(N_HEADS_C)]

        def scores(h):
            st[h]["z"] = lax.dot_general(qh[h // 2][h % 2], kbs[h // 2], _NT,
                                         preferred_element_type=_F32)

        def logs(h):
            z = st[h].pop("z")
            log_beta = jnp.minimum(z, 0.0) - LOG2E * jnp.log(1.0 + jnp.exp2(-jnp.abs(z)))
            lom = log_beta - z
            if on_diagonal:
                lom = jnp.where(strict, lom, 0.0)
            hi = lom.astype(_BF16)
            lo = (lom - hi.astype(_F32)).astype(_BF16)
            st[h]["log_beta"] = log_beta
            st[h]["split"] = jnp.concatenate([hi, lo], axis=1)
            st[h]["r_new"] = rs[h] + jnp.sum(lom, axis=-1, keepdims=True)

        def suffix(h):
            st[h]["suffix"] = jnp.dot(st[h].pop("split"), tri, preferred_element_type=_F32)

        def weights(h):
            a = jnp.exp2(st[h].pop("log_beta") + st[h].pop("suffix") + rs[h])
            if on_diagonal:
                a = jnp.where(strict, a, 0.0)
            st[h]["a"] = a.astype(_BF16)

        def values(h):
            st[h]["pv"] = jnp.dot(st[h].pop("a"), vbs[h // 2], preferred_element_type=_F32)

        _emit_skewed((scores, logs, suffix, weights, values), N_HEADS_C)
        new_r = tuple(st[h]["r_new"] for h in range(N_HEADS_C))
        new_acc = tuple(
            accs[p] + jnp.where(lane < HEAD_DIM, st[2 * p]["pv"], st[2 * p + 1]["pv"])
            for p in range(n_pairs))
        return new_r, new_acc

    init = (tuple(jnp.zeros((TQ, 1), _F32) for _ in range(N_HEADS_C)),
            tuple(jnp.zeros((TQ, LANES), _F32) for _ in range(n_pairs)))
    def sticks_left(rs):
        r_max = functools.reduce(jnp.maximum, rs)
        return jnp.max(r_max) >= F32_UNDERFLOW_LOG2

    def walk(state):
        i, _, carry = state
        rs, accs = block((qi - 1 - i) * TK, carry, False)
        return i + 1, sticks_left(rs), (rs, accs)

    rs, accs = block(t0, init, True)
    _, _, (_, accs) = lax.while_loop(lambda s: (s[0] < qi) & s[1], walk,
                                     (jnp.int32(0), sticks_left(rs), (rs, accs)))
    for pair in range(n_pairs):
        o_ref[0, :, pair * LANES:(pair + 1) * LANES] = accs[pair].astype(_BF16)


def _attn_c(proj):
    bsz = proj.shape[0]
    return pl.pallas_call(
        _attn_c_body,
        out_shape=jax.ShapeDtypeStruct((bsz, SEQ, WIDTH_C), _BF16),
        grid=(bsz, SEQ // TQ),
        in_specs=_group_specs(QC_COL, KC_COL, VC_COL, WIDTH_C),
        out_specs=pl.BlockSpec((1, TQ, WIDTH_C), lambda b, qi: (b, qi, 0)),
        compiler_params=pltpu.CompilerParams(
            dimension_semantics=("arbitrary", "arbitrary"), vmem_limit_bytes=VMEM_LIMIT),
        name="attn_c",
    )(proj, proj, proj)


def _q_scale_row():
    s = np.ones((1, IN_COLS), np.float32)
    s[:, QA_COL:QA_COL + WIDTH_A] = HEAD_DIM ** -0.5 * LOG2E
    s[:, QB_COL:QB_COL + WIDTH_B] = DIFF_QK_DIM ** -0.5 * LOG2E
    s[:, QC_COL:QC_COL + WIDTH_C] = HEAD_DIM ** -0.5 * LOG2E
    return jnp.asarray(s)


def kernel(x, ffn1_pre_g, ffn1_w_gate, ffn1_w_up, ffn1_w_down, ffn1_post_g, mix_pre_g, w_in, rel_bias, diff_lambda_q1, diff_lambda_k1, diff_lambda_q2, diff_lambda_k2, diff_subln_g, w_out, mix_post_g, ffn2_pre_g, ffn2_w_gate, ffn2_w_up, ffn2_w_down, ffn2_post_g):
    bsz, seq, d = x.shape
    n = bsz * seq
    xf = x.reshape(n, d)
    q_scale = _q_scale_row()
    row = lambda v: v.reshape(1, -1)
    bf16 = lambda w: w.astype(_BF16)
    for l in range(DEPTH):
        lambda_init = 0.8 - 0.6 * math.exp(-0.3 * l)
        ffn1 = (row(ffn1_pre_g[l]), bf16(ffn1_w_gate[l]), bf16(ffn1_w_up[l]),
                bf16(ffn1_w_down[l]), row(ffn1_post_g[l]))
        ffn2 = (row(ffn2_pre_g[l]), bf16(ffn2_w_gate[l]), bf16(ffn2_w_up[l]),
                bf16(ffn2_w_down[l]), row(ffn2_post_g[l]))
        xf, proj = _ffn_inproj(xf, ffn1, row(mix_pre_g[l]), bf16(w_in[l]), q_scale)
        proj = proj.reshape(bsz, seq, IN_COLS)
        rb_pad = jnp.pad(rel_bias[l], ((0, 8 - N_HEADS_A), (0, REL_PAD - (2 * REL_CLIP + 1))))
        ya = _attn_a(proj, rb_pad)
        yb = _attn_b(proj, row(diff_lambda_q1[l]), row(diff_lambda_k1[l]),
                     row(diff_lambda_q2[l]), row(diff_lambda_k2[l]),
                     row(jnp.tile(diff_subln_g[l], N_HEADS_B)), lambda_init)
        yc = _attn_c(proj)
        xf = _outproj_ffn(xf, ya.reshape(n, WIDTH_A), yb.reshape(n, WIDTH_B),
                          yc.reshape(n, WIDTH_C), bf16(w_out[l]), row(mix_post_g[l]), ffn2)
    return xf.reshape(bsz, seq, d)
```

```python
import functools
import math

import numpy as np
import jax
import jax.numpy as jnp
from jax import lax
from jax.experimental import pallas as pl
from jax.experimental.pallas import tpu as pltpu

D_MODEL = 1024
SEQ = 2048
DEPTH = 2
CHUNK = 64
HEAD_DIM = 64
N_HEADS_A = 6
N_HEADS_B = 4
N_HEADS_C = 6
DIFF_QK_DIM = 32
WIDTH_A = N_HEADS_A * HEAD_DIM
WIDTH_B = N_HEADS_B * HEAD_DIM
WIDTH_C = N_HEADS_C * HEAD_DIM
IN_COLS = 3 * WIDTH_A + 3 * WIDTH_B + 3 * WIDTH_C
LEFT_CHUNKS = 8
REL_CLIP = 128
D_FF = 2816
NORM_EPS = 1e-6
NEG_INF = -1e30
FFN_RESIDUAL_WEIGHT = 0.5
LOG2E = math.log2(math.e)
F32_UNDERFLOW_LOG2 = -152.0

LANES = 128
QA_COL, KA_COL, VA_COL = 0, WIDTH_A, 2 * WIDTH_A
QB_COL, KB_COL, VB_COL = 3 * WIDTH_A, 3 * WIDTH_A + WIDTH_B, 3 * WIDTH_A + 2 * WIDTH_B
QC_COL = 3 * WIDTH_A + 3 * WIDTH_B
KC_COL, VC_COL = QC_COL + WIDTH_C, QC_COL + 2 * WIDTH_C

TM = 512
FF_CHUNK = 256
TQ = 256
TK = 256
SPAN = 512
BAND_TQ = 128
BAND_PAD = LEFT_CHUNKS * CHUNK
BAND_WIN = BAND_PAD + BAND_TQ
REL_PAD = 384
ROLL_W = 768
VMEM_LIMIT = 52 * 1024 * 1024

_F32 = jnp.float32
_BF16 = jnp.bfloat16
_NT = (((1,), (1,)), ((), ()))


def _rms(x, g):
    ms = jnp.mean(x * x, axis=-1, keepdims=True)
    return x * lax.rsqrt(ms + NORM_EPS) * g


def _emit_skewed(stages, n_items):
    for step in range(n_items + len(stages) - 1):
        for k, stage in enumerate(stages):
            if 0 <= step - k < n_items:
                stage(step - k)


def _head_halves(q):
    lane = lax.broadcasted_iota(jnp.int32, q.shape, 1)
    zero = jnp.zeros_like(q)
    return jnp.where(lane < HEAD_DIM, q, zero), jnp.where(lane >= HEAD_DIM, q, zero)


def _swiglu_half_step(x, pre_g_ref, wg_ref, wu_ref, wd_ref, post_g_ref, acc_ref):
    h = _rms(x, pre_g_ref[...]).astype(_BF16)
    for c in range(D_FF // FF_CHUNK):
        sl = slice(c * FF_CHUNK, (c + 1) * FF_CHUNK)
        g = jnp.dot(h, wg_ref[:, sl], preferred_element_type=_F32)
        u = jnp.dot(h, wu_ref[:, sl], preferred_element_type=_F32)
        a = (g * jax.nn.sigmoid(g) * u).astype(_BF16)
        d = jnp.dot(a, wd_ref[sl, :], preferred_element_type=_F32)
        if c == 0:
            acc_ref[...] = d
        else:
            acc_ref[...] += d
    return x + FFN_RESIDUAL_WEIGHT * _rms(acc_ref[...], post_g_ref[...])


def _resident(shape):
    return pl.BlockSpec(shape, lambda *_: (0,) * len(shape), pipeline_mode=pl.Buffered(1))


def _row_tile(width):
    return pl.BlockSpec((TM, width), lambda i: (i, 0))


_FFN_WEIGHT_SPECS = [_resident((1, D_MODEL)), _resident((D_MODEL, D_FF)),
                     _resident((D_MODEL, D_FF)), _resident((D_FF, D_MODEL)),
                     _resident((1, D_MODEL))]
_DENSE_PARAMS = pltpu.CompilerParams(dimension_semantics=("arbitrary",),
                                     vmem_limit_bytes=VMEM_LIMIT)


def _ffn_inproj_body(x_ref, pre_g_ref, wg_ref, wu_ref, wd_ref, post_g_ref,
                     mix_g_ref, w_in_ref, scale_ref, x_out_ref, proj_ref, acc_ref):
    x1 = _swiglu_half_step(x_ref[...], pre_g_ref, wg_ref, wu_ref, wd_ref, post_g_ref, acc_ref)
    x_out_ref[...] = x1
    h = _rms(x1, mix_g_ref[...]).astype(_BF16)
    cw = 512
    for c in range(IN_COLS // cw):
        sl = slice(c * cw, (c + 1) * cw)
        p = jnp.dot(h, w_in_ref[:, sl], preferred_element_type=_F32)
        proj_ref[:, sl] = (p * scale_ref[:, sl]).astype(_BF16)


def _ffn_inproj(x, ffn_weights, mix_g, w_in, scale):
    n = x.shape[0]
    return pl.pallas_call(
        _ffn_inproj_body,
        out_shape=(jax.ShapeDtypeStruct((n, D_MODEL), _F32),
                   jax.ShapeDtypeStruct((n, IN_COLS), _BF16)),
        grid=(n // TM,),
        in_specs=[_row_tile(D_MODEL)] + _FFN_WEIGHT_SPECS
        + [_resident((1, D_MODEL)), _resident((D_MODEL, IN_COLS)), _resident((1, IN_COLS))],
        out_specs=(_row_tile(D_MODEL), _row_tile(IN_COLS)),
        scratch_shapes=[pltpu.VMEM((TM, D_MODEL), _F32)],
        compiler_params=_DENSE_PARAMS,
        name="ffn_inproj",
    )(x, *ffn_weights, mix_g, w_in, scale)


def _outproj_ffn_body(x_ref, ya_ref, yb_ref, yc_ref, w_out_ref, mix_post_g_ref,
                      pre_g_ref, wg_ref, wu_ref, wd_ref, post_g_ref, o_ref, acc_ref):
    y = jnp.concatenate([ya_ref[...], yb_ref[...], yc_ref[...]], axis=-1)
    m = jnp.dot(y, w_out_ref[...], preferred_element_type=_F32)
    x1 = x_ref[...] + _rms(m, mix_post_g_ref[...])
    o_ref[...] = _swiglu_half_step(x1, pre_g_ref, wg_ref, wu_ref, wd_ref, post_g_ref, acc_ref)


def _outproj_ffn(x, ya, yb, yc, w_out, mix_post_g, ffn_weights):
    n = x.shape[0]
    return pl.pallas_call(
        _outproj_ffn_body,
        out_shape=jax.ShapeDtypeStruct((n, D_MODEL), _F32),
        grid=(n // TM,),
        in_specs=[_row_tile(D_MODEL), _row_tile(WIDTH_A), _row_tile(WIDTH_B), _row_tile(WIDTH_C),
                  _resident((D_MODEL, D_MODEL)), _resident((1, D_MODEL))] + _FFN_WEIGHT_SPECS,
        out_specs=_row_tile(D_MODEL),
        scratch_shapes=[pltpu.VMEM((TM, D_MODEL), _F32)],
        compiler_params=_DENSE_PARAMS,
        name="outproj_ffn",
    )(x, ya, yb, yc, w_out, mix_post_g, *ffn_weights)


def _group_specs(q_col, k_col, v_col, width):
    assert q_col % width == 0 and k_col % width == 0 and v_col % width == 0
    qb, kb, vb = q_col // width, k_col // width, v_col // width
    return [
        pl.BlockSpec((1, TQ, width), lambda b, qi: (b, qi, qb)),
        pl.BlockSpec((1, SEQ, width), lambda b, qi: (b, 0, kb)),
        pl.BlockSpec((1, SEQ, width), lambda b, qi: (b, 0, vb)),
    ]


def _attn_a_body(rb_ref, q_ref, k_ref, v_ref, o_ref, kpad, vpad, bias, rext):
    b = pl.program_id(0)
    qi = pl.program_id(1)

    @pl.when((b == 0) & (qi == 0))
    def _build_bias():
        m_i = lax.broadcasted_iota(jnp.int32, (REL_PAD, ROLL_W), 0)
        j_i = lax.broadcasted_iota(jnp.int32, (REL_PAD, ROLL_W), 1)
        idx = jnp.where(j_i < BAND_WIN,
                        jnp.clip(BAND_PAD - j_i, -REL_CLIP, REL_CLIP) + REL_CLIP,
                        2 * REL_CLIP)
        sel = jnp.where(m_i == idx, 1.0, 0.0).astype(_BF16)
        rb = rb_ref[...]
        hi = rb.astype(_BF16)
        r1 = rb - hi.astype(_F32)
        mid = r1.astype(_BF16)
        lo = (r1 - mid.astype(_F32)).astype(_BF16)
        rext[...] = LOG2E * (jnp.dot(hi, sel, preferred_element_type=_F32)
                             + jnp.dot(mid, sel, preferred_element_type=_F32)
                             + jnp.dot(lo, sel, preferred_element_type=_F32))
        qc = lax.broadcasted_iota(jnp.int32, (BAND_TQ, BAND_WIN), 0) // CHUNK
        kc = lax.broadcasted_iota(jnp.int32, (BAND_TQ, BAND_WIN), 1) // CHUNK
        d = kc - qc
        in_band = (d >= 0) & (d <= LEFT_CHUNKS)
        for h in range(N_HEADS_A):
            full = jnp.broadcast_to(rext[h:h + 1, :], (BAND_TQ, ROLL_W))
            toep = pltpu.roll(full, 0, 1, stride=1, stride_axis=0)
            bias[h] = jnp.where(in_band, toep[:, :BAND_WIN], NEG_INF)

    @pl.when(qi == 0)
    def _stage_keys():
        zeros = jnp.zeros((BAND_PAD, WIDTH_A), _BF16)
        kpad[0:BAND_PAD, :] = zeros
        vpad[0:BAND_PAD, :] = zeros
        kpad[BAND_PAD:, :] = k_ref[0]
        vpad[BAND_PAD:, :] = v_ref[0]

    lane = lax.broadcasted_iota(jnp.int32, (BAND_TQ, LANES), 1)
    col = lax.broadcasted_iota(jnp.int32, (1, BAND_WIN), 1)
    n_sub = TQ // BAND_TQ
    n_pairs = N_HEADS_A // 2
    t0s = [pl.multiple_of(qi * TQ + sub * BAND_TQ, BAND_TQ) for sub in range(n_sub)]
    pad_bias = [jnp.where(col >= BAND_PAD - t0, 0.0, NEG_INF) for t0 in t0s]
    st = [dict() for _ in range(n_sub * N_HEADS_A)]

    def unpack(i):
        sub, head = divmod(i, N_HEADS_A)
        pair = head // 2
        return sub, head, slice(pair * LANES, (pair + 1) * LANES)

    def scores(i):
        sub, head, cols = unpack(i)
        q = q_ref[0, sub * BAND_TQ:(sub + 1) * BAND_TQ, cols]
        kwin = kpad[pl.ds(t0s[sub], BAND_WIN), cols]
        st[i]["s"] = lax.dot_general(_head_halves(q)[head % 2], kwin, _NT,
                                     preferred_element_type=_F32)

    def softmax(i):
        sub, head, _ = unpack(i)
        s = st[i].pop("s") + bias[head] + pad_bias[sub]
        st[i]["p"] = jnp.exp2(s - jnp.max(s, axis=-1, keepdims=True)).astype(_BF16)

    ones = jnp.ones((BAND_WIN, LANES), _BF16)

    def values(i):
        sub, head, cols = unpack(i)
        vwin = jnp.concatenate([vpad[pl.ds(t0s[sub], BAND_WIN), cols], ones], axis=1)
        o = jnp.dot(st[i].pop("p"), vwin, preferred_element_type=_F32)
        o = o[:, :LANES] / o[:, LANES:]
        if head % 2 == 0:
            st[i]["o"] = o
        else:
            rows = slice(sub * BAND_TQ, (sub + 1) * BAND_TQ)
            o_ref[0, rows, cols] = jnp.where(lane < HEAD_DIM, st[i - 1].pop("o"), o).astype(_BF16)

    _emit_skewed((scores, softmax, values), n_sub * N_HEADS_A)


def _attn_a(proj, rb_pad):
    bsz = proj.shape[0]
    return pl.pallas_call(
        _attn_a_body,
        out_shape=jax.ShapeDtypeStruct((bsz, SEQ, WIDTH_A), _BF16),
        grid=(bsz, SEQ // TQ),
        in_specs=[pl.BlockSpec((8, REL_PAD), lambda b, qi: (0, 0))]
        + _group_specs(QA_COL, KA_COL, VA_COL, WIDTH_A),
        out_specs=pl.BlockSpec((1, TQ, WIDTH_A), lambda b, qi: (b, qi, 0)),
        scratch_shapes=[
            pltpu.VMEM((SEQ + BAND_PAD, WIDTH_A), _BF16),
            pltpu.VMEM((SEQ + BAND_PAD, WIDTH_A), _BF16),
            pltpu.VMEM((N_HEADS_A, BAND_TQ, BAND_WIN), _F32),
            pltpu.VMEM((8, ROLL_W), _F32),
        ],
        compiler_params=pltpu.CompilerParams(
            dimension_semantics=("arbitrary", "arbitrary"), vmem_limit_bytes=VMEM_LIMIT),
        name="attn_a",
    )(rb_pad, proj, proj, proj)


def _attn_b_body(lq1_ref, lk1_ref, lq2_ref, lk2_ref, g_ref,
                 q0_ref, k0_ref, v0_ref, q1_ref, k1_ref, v1_ref, o_ref,
                 vt_ref, ramp_ref, *, lambda_init):
    q_refs, k_refs, v_refs = (q0_ref, q1_ref), (k0_ref, k1_ref), (v0_ref, v1_ref)
    b = pl.program_id(0)
    qi = pl.program_id(1)
    t0 = qi * TQ
    n_left = t0 // SPAN
    n_pairs = N_HEADS_B // 2
    n_maps = 2 * N_HEADS_B
    slopes = [LOG2E * 2.0 ** (-2 * (h + 1)) for h in range(N_HEADS_B)]

    @pl.when((b == 0) & (qi == 0))
    def _build_ramps():
        j = lax.broadcasted_iota(jnp.int32, (SPAN, TQ), 0).astype(_F32)
        for h in range(N_HEADS_B):
            ramp_ref[h] = slopes[h] * j

    @pl.when(qi == 0)
    def _stage_values():
        for pair in range(n_pairs):
            vt_ref[pair] = v_refs[pair][0].astype(_F32).T.astype(_BF16)

    lam = (jnp.exp(jnp.sum(lq1_ref[...] * lk1_ref[...], axis=-1, keepdims=True))
           - jnp.exp(jnp.sum(lq2_ref[...] * lk2_ref[...], axis=-1, keepdims=True))
           + lambda_init)

    lane = lax.broadcasted_iota(jnp.int32, (TQ, LANES), 1)
    qmaps = []
    for pair in range(n_pairs):
        q = q_refs[pair][0]
        qmaps.append([jnp.where((lane // DIFF_QK_DIM) == i, q, jnp.zeros_like(q))
                      for i in range(4)])

    def span(s0, carry, width, on_diagonal):
        ms, ls, accs = carry
        s0 = pl.multiple_of(s0, width)
        if on_diagonal:
            key_s = s0 + lax.broadcasted_iota(jnp.int32, (width, TQ), 0)
            qry_t = t0 + lax.broadcasted_iota(jnp.int32, (width, TQ), 1)
            allowed = (key_s // CHUNK) <= (qry_t // CHUNK)
            shifted = (qry_t - jnp.abs(qry_t - key_s)).astype(_F32)
        kbs = [k_refs[pair][0, pl.ds(s0, width), :] for pair in range(n_pairs)]
        vts = [vt_ref[pair, :, pl.ds(s0, width)] for pair in range(n_pairs)]
        st = [dict() for _ in range(n_maps)]

        def scores(i):
            st[i]["s"] = lax.dot_general(kbs[i // 4], qmaps[i // 4][i % 4], _NT,
                                         preferred_element_type=_F32)

        def maxima(i):
            h = i // 2
            if on_diagonal:
                u = jnp.where(allowed, st[i].pop("s") + slopes[h] * shifted, NEG_INF)
                base = 0.0
            else:
                u = st[i].pop("s") + ramp_ref[h, 0:width, :]
                base = slopes[h] * s0.astype(_F32)
            m_new = jnp.maximum(ms[i], jnp.max(u, axis=0, keepdims=True) + base)
            st[i].update(u=u, m=m_new, ref=m_new - base, alpha=jnp.exp2(ms[i] - m_new))

        def weights(i):
            p = jnp.exp2(st[i].pop("u") - st[i].pop("ref"))
            st[i].update(p=p.astype(_BF16),
                         l=st[i]["alpha"] * ls[i] + jnp.sum(p, axis=0, keepdims=True))

        def values(i):
            st[i]["acc"] = st[i].pop("alpha") * accs[i] + jnp.dot(
                vts[i // 4], st[i].pop("p"), preferred_element_type=_F32)

        _emit_skewed((scores, maxima, weights, values), n_maps)
        return (tuple(s["m"] for s in st), tuple(s["l"] for s in st),
                tuple(s["acc"] for s in st))

    init = (tuple(jnp.full((1, TQ), NEG_INF, _F32) for _ in range(n_maps)),
            tuple(jnp.zeros((1, TQ), _F32) for _ in range(n_maps)),
            tuple(jnp.zeros((LANES, TQ), _F32) for _ in range(n_maps)))
    carry = lax.fori_loop(0, n_left, lambda j, c: span(j * SPAN, c, SPAN, False), init)
    _, ls, accs = span(n_left * SPAN, carry, SPAN, True)
    outs = [acc / l for acc, l in zip(accs, ls)]

    dim = lax.broadcasted_iota(jnp.int32, (LANES, TQ), 0)
    lo_half = dim < HEAD_DIM
    for pair in range(n_pairs):
        heads = []
        for h in range(2):
            i0 = 4 * pair + 2 * h
            heads.append(outs[i0] - lam * outs[i0 + 1])
        o = jnp.where(lo_half, heads[0], heads[1])
        o2 = o * o
        ss_lo = jnp.sum(jnp.where(lo_half, o2, 0.0), axis=0, keepdims=True)
        ss_hi = jnp.sum(jnp.where(lo_half, 0.0, o2), axis=0, keepdims=True)
        ms = jnp.where(lo_half, ss_lo, ss_hi) * (1.0 / HEAD_DIM)
        cols = slice(pair * LANES, (pair + 1) * LANES)
        y = (o * lax.rsqrt(ms + NORM_EPS)).T * g_ref[:, cols]
        o_ref[0, :, cols] = (y * (1.0 - lambda_init)).astype(_BF16)


def _attn_b(proj, lq1, lk1, lq2, lk2, g4, lambda_init):
    bsz = proj.shape[0]
    small = lambda w: pl.BlockSpec((1, w), lambda b, qi: (0, 0))
    return pl.pallas_call(
        functools.partial(_attn_b_body, lambda_init=lambda_init),
        out_shape=jax.ShapeDtypeStruct((bsz, SEQ, WIDTH_B), _BF16),
        grid=(bsz, SEQ // TQ),
        in_specs=[small(DIFF_QK_DIM), small(DIFF_QK_DIM), small(DIFF_QK_DIM),
                  small(DIFF_QK_DIM), small(WIDTH_B)]
        + _group_specs(QB_COL, KB_COL, VB_COL, LANES)
        + _group_specs(QB_COL + LANES, KB_COL + LANES, VB_COL + LANES, LANES),
        out_specs=pl.BlockSpec((1, TQ, WIDTH_B), lambda b, qi: (b, qi, 0)),
        scratch_shapes=[
            pltpu.VMEM((N_HEADS_B // 2, LANES, SEQ), _BF16),
            pltpu.VMEM((N_HEADS_B, SPAN, TQ), _F32),
        ],
        compiler_params=pltpu.CompilerParams(
            dimension_semantics=("arbitrary", "arbitrary"), vmem_limit_bytes=VMEM_LIMIT),
        name="attn_b",
    )(lq1, lk1, lq2, lk2, g4, *([proj] * 6))


def _attn_c_body(q_ref, k_ref, v_ref, o_ref):
    qi = pl.program_id(1)
    t0 = qi * TQ

    n_pairs = N_HEADS_C // 2
    lane = lax.broadcasted_iota(jnp.int32, (TQ, LANES), 1)
    qh = [_head_halves(q_ref[0, :, pair * LANES:(pair + 1) * LANES]) for pair in range(n_pairs)]
    tri_blk = jnp.where(lax.broadcasted_iota(jnp.int32, (TK, TK), 0)
                        > lax.broadcasted_iota(jnp.int32, (TK, TK), 1), 1.0, 0.0).astype(_BF16)
    tri = tri_blk

    def block(s0, carry, on_diagonal):
        rs, accs = carry
        s0 = pl.multiple_of(s0, TK)
        if on_diagonal:
            strict = (lax.broadcasted_iota(jnp.int32, (TQ, TK), 1)
                      < lax.broadcasted_iota(jnp.int32, (TQ, TK), 0))
        kbs = [k_ref[0, pl.ds(s0, TK), p * LANES:(p + 1) * LANES] for p in range(n_pairs)]
        vbs = [v_ref[0, pl.ds(s0, TK), p * LANES:(p + 1) * LANES] for p in range(n_pairs)]
        st = [dict() for _ in range(N_HEADS_C)]

        def scores(h):
            st[h]["z"] = lax.dot_general(qh[h // 2][h % 2], kbs[h // 2], _NT,
                                         preferred_element_type=_F32)

        def logs(h):
            z = st[h].pop("z")
            log_beta = jnp.minimum(z, 0.0) - LOG2E * jnp.log(1.0 + jnp.exp2(-jnp.abs(z)))
            lom = log_beta - z
            if on_diagonal:
                lom = jnp.where(strict, lom, 0.0)
            st[h]["log_beta"] = log_beta
            st[h]["split"] = lom.astype(_BF16)
            st[h]["r_new"] = rs[h] + jnp.sum(lom, axis=-1, keepdims=True)

        def suffix(h):
            st[h]["suffix"] = jnp.dot(st[h].pop("split"), tri, preferred_element_type=_F32)

        def weights(h):
            a = jnp.exp2(st[h].pop("log_beta") + st[h].pop("suffix") + rs[h])
            if on_diagonal:
                a = jnp.where(strict, a, 0.0)
            st[h]["a"] = a.astype(_BF16)

        def values(h):
            st[h]["pv"] = jnp.dot(st[h].pop("a"), vbs[h // 2], preferred_element_type=_F32)

        _emit_skewed((scores, logs, suffix, weights, values), N_HEADS_C)
        new_r = tuple(st[h]["r_new"] for h in range(N_HEADS_C))
        new_acc = tuple(
            accs[p] + jnp.where(lane < HEAD_DIM, st[2 * p]["pv"], st[2 * p + 1]["pv"])
            for p in range(n_pairs))
        return new_r, new_acc

    init = (tuple(jnp.zeros((TQ, 1), _F32) for _ in range(N_HEADS_C)),
            tuple(jnp.zeros((TQ, LANES), _F32) for _ in range(n_pairs)))
    def sticks_left(rs):
        r_max = functools.reduce(jnp.maximum, rs)
        return jnp.max(r_max) >= F32_UNDERFLOW_LOG2

    def walk(state):
        i, _, carry = state
        rs, accs = block((qi - 1 - i) * TK, carry, False)
        return i + 1, sticks_left(rs), (rs, accs)

    rs, accs = block(t0, init, True)
    _, _, (_, accs) = lax.while_loop(lambda s: (s[0] < qi) & s[1], walk,
                                     (jnp.int32(0), sticks_left(rs), (rs, accs)))
    for pair in range(n_pairs):
        o_ref[0, :, pair * LANES:(pair + 1) * LANES] = accs[pair].astype(_BF16)


def _attn_c(proj):
    bsz = proj.shape[0]
    return pl.pallas_call(
        _attn_c_body,
        out_shape=jax.ShapeDtypeStruct((bsz, SEQ, WIDTH_C), _BF16),
        grid=(bsz, SEQ // TQ),
        in_specs=_group_specs(QC_COL, KC_COL, VC_COL, WIDTH_C),
        out_specs=pl.BlockSpec((1, TQ, WIDTH_C), lambda b, qi: (b, qi, 0)),
        compiler_params=pltpu.CompilerParams(
            dimension_semantics=("arbitrary", "arbitrary"), vmem_limit_bytes=VMEM_LIMIT),
        name="attn_c",
    )(proj, proj, proj)


def _q_scale_row():
    s = np.ones((1, IN_COLS), np.float32)
    s[:, QA_COL:QA_COL + WIDTH_A] = HEAD_DIM ** -0.5 * LOG2E
    s[:, QB_COL:QB_COL + WIDTH_B] = DIFF_QK_DIM ** -0.5 * LOG2E
    s[:, QC_COL:QC_COL + WIDTH_C] = HEAD_DIM ** -0.5 * LOG2E
    return jnp.asarray(s)


def kernel(x, ffn1_pre_g, ffn1_w_gate, ffn1_w_up, ffn1_w_down, ffn1_post_g, mix_pre_g, w_in, rel_bias, diff_lambda_q1, diff_lambda_k1, diff_lambda_q2, diff_lambda_k2, diff_subln_g, w_out, mix_post_g, ffn2_pre_g, ffn2_w_gate, ffn2_w_up, ffn2_w_down, ffn2_post_g):
    bsz, seq, d = x.shape
    n = bsz * seq
    xf = x.reshape(n, d)
    q_scale = _q_scale_row()
    row = lambda v: v.reshape(1, -1)
    bf16 = lambda w: w.astype(_BF16)
    for l in range(DEPTH):
        lambda_init = 0.8 - 0.6 * math.exp(-0.3 * l)
        ffn1 = (row(ffn1_pre_g[l]), bf16(ffn1_w_gate[l]), bf16(ffn1_w_up[l]),
                bf16(ffn1_w_down[l]), row(ffn1_post_g[l]))
        ffn2 = (row(ffn2_pre_g[l]), bf16(ffn2_w_gate[l]), bf16(ffn2_w_up[l]),
                bf16(ffn2_w_down[l]), row(ffn2_post_g[l]))
        xf, proj = _ffn_inproj(xf, ffn1, row(mix_pre_g[l]), bf16(w_in[l]), q_scale)
        proj = proj.reshape(bsz, seq, IN_COLS)
        rb_pad = jnp.pad(rel_bias[l], ((0, 8 - N_HEADS_A), (0, REL_PAD - (2 * REL_CLIP + 1))))
        ya = _attn_a(proj, rb_pad)
        yb = _attn_b(proj, row(diff_lambda_q1[l]), row(diff_lambda_k1[l]),
                     row(diff_lambda_q2[l]), row(diff_lambda_k2[l]),
                     row(jnp.tile(diff_subln_g[l], N_HEADS_B)), lambda_init)
        yc = _attn_c(proj)
        xf = _outproj_ffn(xf, ya.reshape(n, WIDTH_A), yb.reshape(n, WIDTH_B),
                          yc.reshape(n, WIDTH_C), bf16(w_out[l]), row(mix_post_g[l]), ffn2)
    return xf.reshape(bsz, seq, d)
```
